```python
import math
import jax
import jax.numpy as jnp
from jax import lax
import numpy as np

D_MODEL = 2048
BATCH = 4
SEQ = 4096
DEPTH = 2
DEC_BATCH = 128
DEC_SEQ = 8
PAST_LEN = 16384
PAGE_SIZE = 128

N_BRANCH = 4
BRANCH_W = D_MODEL // 4
MLA_HEADS = 8
MLA_D_NOPE = 64
MLA_D_ROPE = 32
MLA_D_QK = MLA_D_NOPE + MLA_D_ROPE
MLA_D_V = BRANCH_W // MLA_HEADS
MLA_Q_RANK = 3 * D_MODEL // 16
MLA_KV_RANK = D_MODEL // 16
MLA_ROW = MLA_KV_RANK + MLA_D_ROPE
ROPE_BASE = 10000.0
POOL_WINDOWS = (2, 4, 8, 16)
POOL_GROUP = BRANCH_W // len(POOL_WINDOWS)
POOL_PAD = max(POOL_WINDOWS) - 1
DIFF_HEADS = 4
DIFF_D = 64
DIFF_DV = 2 * DIFF_D
DIFF_ROW = 2 * DIFF_D + DIFF_DV
SB_HEADS = 8
SB_D = BRANCH_W // SB_HEADS
SB_ROW = 2 * SB_D
REL_BUCKETS = 32
REL_MAX_DIST = 128
D_FF = -(-8 * D_MODEL // (3 * 256)) * 256
QBLK = 128
EPS = 1e-6
IN_SPLITS = (MLA_Q_RANK, MLA_KV_RANK, MLA_D_ROPE, BRANCH_W,
             DIFF_HEADS * 2 * DIFF_D, 2 * DIFF_D, DIFF_DV,
             SB_HEADS * SB_D, SB_D, SB_D, N_BRANCH * D_MODEL)
D_IN = sum(IN_SPLITS)

kernel_name = 'hybrid_mla_pool_diff_stickbreak_step'


def rmsnorm(x, g):
    xf = x.astype(jnp.float32)
    y = xf * lax.rsqrt(jnp.mean(xf * xf, axis=-1, keepdims=True) + EPS)
    return (y * g.astype(jnp.float32)).astype(x.dtype)


def rope(x, pos):
    half = x.shape[-1] // 2
    inv = ROPE_BASE ** (-jnp.arange(half, dtype=jnp.float32) / half)
    ang = pos.astype(jnp.float32)[:, None] * inv[None, :]
    shp = (pos.shape[0],) + (1,) * (x.ndim - 3) + (half,)
    cos, sin = jnp.cos(ang).reshape(shp), jnp.sin(ang).reshape(shp)
    xf = x.astype(jnp.float32)
    x1, x2 = xf[..., :half], xf[..., half:]
    return jnp.concatenate([x1 * cos - x2 * sin, x1 * sin + x2 * cos], axis=-1).astype(x.dtype)


def rel_bucket(qpos, kpos):
    n = jnp.maximum(qpos[:, None] - kpos[None, :], 0)
    exact = REL_BUCKETS // 2
    nf = jnp.maximum(n, 1).astype(jnp.float32)
    large = exact + (jnp.log(nf / exact) / math.log(REL_MAX_DIST / exact)
                     * (REL_BUCKETS - exact)).astype(jnp.int32)
    return jnp.where(n < exact, n, jnp.minimum(large, REL_BUCKETS - 1))


def mla_attend(q_lat, q_rope, rows, kscale, qpos, kpos):
    c, kr = rows[..., :MLA_KV_RANK], rows[..., MLA_KV_RANK:]
    s = jnp.einsum('bqhc,bkc->bhqk', q_lat, c) + jnp.einsum('bqhr,bkr->bhqk', q_rope, kr)
    s = s.astype(jnp.float32) * jnp.swapaxes(kscale, 1, 2).astype(jnp.float32)[:, :, None, :] * (MLA_D_QK ** -0.5)
    mask = kpos[None, :] <= qpos[:, None]
    p = jax.nn.softmax(jnp.where(mask, s, -jnp.inf), axis=-1)
    return jnp.einsum('bhqk,bkc->bqhc', p.astype(c.dtype), c)


def diff_attend(q, rows, lam, rel_bias, qpos, kpos):
    b, tk = rows.shape[0], rows.shape[1]
    k = rows[..., :2 * DIFF_D].reshape(b, tk, 2, DIFF_D)
    v = rows[..., 2 * DIFF_D:]
    bias = jnp.transpose(rel_bias[rel_bucket(qpos, kpos)], (2, 0, 1)).astype(jnp.float32)
    s = jnp.einsum('bqhid,bkid->bihqk', q, k).astype(jnp.float32) * (DIFF_D ** -0.5) + bias
    mask = kpos[None, :] <= qpos[:, None]
    p = jax.nn.softmax(jnp.where(mask, s, -jnp.inf), axis=-1)
    wts = p[:, 0] - lam * p[:, 1]
    return jnp.einsum('bhqk,bkv->bqhv', wts.astype(v.dtype), v)


def sb_attend(q, rows, qpos, kpos):
    k, v = rows[..., :SB_D], rows[..., SB_D:]
    z = jnp.einsum('bqhd,bkd->bhqk', q, k).astype(jnp.float32) * (SB_D ** -0.5)
    mask = kpos[None, :] < qpos[:, None]
    log_keep = jnp.where(mask, jax.nn.log_sigmoid(-z), 0.0)
    between = lax.cumsum(log_keep, axis=3, reverse=True) - log_keep
    a = jnp.where(mask, jnp.exp(jax.nn.log_sigmoid(z) + between), 0.0)
    return jnp.einsum('bhqk,bkd->bqhd', a.astype(v.dtype), v)


def pool_mix(u, prefix, pos, w_groups, scale):
    b, t, c = u.shape
    ext = jnp.concatenate([prefix, u], axis=1)
    ext32 = ext.astype(jnp.float32)
    c0 = jnp.concatenate([jnp.zeros((b, 1, c), jnp.float32), jnp.cumsum(ext32, axis=1)], axis=1)
    u32 = ext32[:, POOL_PAD:]
    outs = []
    for g, w in enumerate(POOL_WINDOWS):
        sl = slice(g * POOL_GROUP, (g + 1) * POOL_GROUP)
        wsum = c0[:, POOL_PAD + 1:POOL_PAD + 1 + t, sl] - c0[:, POOL_PAD + 1 - w:POOL_PAD + 1 - w + t, sl]
        cnt = jnp.minimum(pos + 1, w).astype(jnp.float32)[None, :, None]
        outs.append(wsum / cnt - u32[..., sl])
    pooled = jnp.concatenate(outs, axis=-1).reshape(b, t, len(POOL_WINDOWS), POOL_GROUP).astype(u.dtype)
    mixed = jnp.einsum('btgi,gio->btgo', pooled, w_groups).reshape(b, t, c) * scale
    return mixed, ext[:, -POOL_PAD:]


def project_tokens(xn, pos, l, P):
    b, t, _ = xn.shape
    proj = xn @ P['w_in'][l]
    offs = [int(i) for i in np.cumsum(IN_SPLITS)[:-1]]
    cq, ckv, kr, pool_u, dq, dk, dv, sq, sk, sv, gates = jnp.split(proj, offs, axis=-1)
    cq = rmsnorm(cq, P['mla_cq_g'][l])
    ckv = rmsnorm(ckv, P['mla_ckv_g'][l])
    q = rmsnorm(jnp.einsum('btc,chd->bthd', cq, P['mla_w_uq'][l]), P['mla_qn_g'][l])
    kn_g = P['mla_kn_g'][l]
    q_lat = jnp.einsum('bthd,chd->bthc', q[..., :MLA_D_NOPE] * kn_g[:MLA_D_NOPE], P['mla_w_uk'][l])
    q_rope = rope(q[..., MLA_D_NOPE:], pos)
    k_nope = jnp.einsum('btc,chd->bthd', ckv, P['mla_w_uk'][l])
    ms = (jnp.sum(jnp.square(k_nope.astype(jnp.float32)), axis=-1)
          + jnp.sum(jnp.square(kr.astype(jnp.float32)), axis=-1, keepdims=True)) / MLA_D_QK
    kscale = lax.rsqrt(ms + EPS).astype(xn.dtype)
    mla_rows = jnp.concatenate([ckv, rope(kr * kn_g[MLA_D_NOPE:], pos)], axis=-1)
    diff_q = rmsnorm(dq.reshape(b, t, DIFF_HEADS, 2, DIFF_D), P['diff_qn_g'][l])
    dk = rmsnorm(dk.reshape(b, t, 2, DIFF_D), P['diff_kn_g'][l]).reshape(b, t, 2 * DIFF_D)
    diff_rows = jnp.concatenate([dk, dv], axis=-1)
    sb_q = sq.reshape(b, t, SB_HEADS, SB_D)
    sb_rows = jnp.concatenate([sk, sv], axis=-1)
    return {'q_lat': q_lat, 'q_rope': q_rope, 'mla_rows': mla_rows, 'kscale': kscale,
            'pool_u': pool_u, 'diff_q': diff_q, 'diff_rows': diff_rows,
            'sb_q': sb_q, 'sb_rows': sb_rows, 'gates': gates}


def diff_lambda(l, P):
    lq1, lk1, lq2, lk2 = P['diff_lambda'][l].astype(jnp.float32)
    lam_init = 0.8 - 0.6 * math.exp(-0.3 * l)
    return jnp.exp(jnp.sum(lq1 * lk1)) - jnp.exp(jnp.sum(lq2 * lk2)) + lam_init, lam_init


def over_query_blocks(fn, qs, qpos):
    b = qs[0].shape[0]
    nblk = qpos.shape[0] // QBLK
    def split(a):
        return jnp.swapaxes(a.reshape(b, nblk, QBLK, *a.shape[2:]), 0, 1)
    def merge(a):
        return jnp.swapaxes(a, 0, 1).reshape(b, nblk * QBLK, *a.shape[3:])
    out = lax.map(lambda xs: fn(*xs[0], xs[1]), (tuple(split(a) for a in qs), qpos.reshape(nblk, QBLK)))
    return tuple(merge(o) for o in out)


def run_layers(x, pos, P, attend, pool_prefix):
    b, t, _ = x.shape
    h = x
    mla_new, ks_new, diff_new, sb_new, pool_new = [], [], [], [], []
    for l in range(DEPTH):
        xn = rmsnorm(h, P['norm1_g'][l])
        tk = project_tokens(xn, pos, l, P)
        lam, lam_init = diff_lambda(l, P)
        o_lat, o_diff, o_sb = attend(l, tk, lam)
        o_mla = jnp.einsum('bthc,chd->bthd', o_lat, P['mla_w_uv'][l]).reshape(b, t, BRANCH_W)
        o_diff = (rmsnorm(o_diff, P['diff_subln_g'][l]) * (1.0 - lam_init)).reshape(b, t, BRANCH_W)
        o_sb = o_sb.reshape(b, t, BRANCH_W)
        o_pool, pool_state = pool_mix(tk['pool_u'], pool_prefix(l), pos, P['pool_w'][l], P['pool_scale'][l])
        merged = 0.0
        for g, o in enumerate((o_mla, o_pool, o_diff, o_sb)):
            gate = jax.nn.sigmoid(tk['gates'][..., g * D_MODEL:(g + 1) * D_MODEL])
            merged = merged + gate * (o @ P['w_branch'][l, g])
        h = h + merged @ P['w_out'][l]
        hn = rmsnorm(h, P['norm2_g'][l])
        h = h + (jax.nn.silu(hn @ P['ffn_w_gate'][l]) * (hn @ P['ffn_w_up'][l])) @ P['ffn_w_down'][l]
        mla_new.append(tk['mla_rows'])
        ks_new.append(tk['kscale'])
        diff_new.append(tk['diff_rows'])
        sb_new.append(tk['sb_rows'])
        pool_new.append(pool_state)
    return (h, jnp.stack(mla_new), jnp.stack(ks_new), jnp.stack(diff_new),
            jnp.stack(sb_new), jnp.stack(pool_new))


def setup_inputs(seed: int = 0) -> dict:
    key = jax.random.key(seed)
    keys = jax.random.split(key, 40)
    counter = iter(range(40))
    f32 = jnp.float32
    def nk():
        return keys[next(counter)]
    def std(shape):
        return jax.random.normal(nk(), shape, f32)
    def nrm(shape, scale):
        return jax.random.normal(nk(), shape, f32) * scale
    def gain(shape):
        return 1.0 + 0.02 * jax.random.normal(nk(), shape, f32)
    n_pages = PAST_LEN // PAGE_SIZE
    n_pool = (DEC_BATCH * n_pages * 5) // 4
    page_table = jax.random.permutation(nk(), n_pool)[:DEC_BATCH * n_pages].reshape(DEC_BATCH, n_pages).astype(jnp.int32)
    cp = (DEPTH, n_pool, PAGE_SIZE)
    return {
        'x_prompt': std((BATCH, SEQ, D_MODEL)),
        'x_sample': std((DEC_BATCH, DEC_SEQ, D_MODEL)),
        'cache_mla': std(cp + (MLA_ROW,)),
        'cache_mla_kscale': jax.random.uniform(nk(), cp + (MLA_HEADS,), f32, 0.7, 1.3),
        'cache_diff': std(cp + (DIFF_ROW,)),
        'cache_sb': std(cp + (SB_ROW,)),
        'state_pool': std((DEPTH, DEC_BATCH, POOL_PAD, BRANCH_W)),
        'page_table': page_table,
        'norm1_g': gain((DEPTH, D_MODEL)),
        'w_in': nrm((DEPTH, D_MODEL, D_IN), D_MODEL ** -0.5),
        'mla_cq_g': gain((DEPTH, MLA_Q_RANK)),
        'mla_ckv_g': gain((DEPTH, MLA_KV_RANK)),
        'mla_w_uq': nrm((DEPTH, MLA_Q_RANK, MLA_HEADS, MLA_D_QK), MLA_Q_RANK ** -0.5),
        'mla_qn_g': gain((DEPTH, MLA_D_QK)),
        'mla_kn_g': gain((DEPTH, MLA_D_QK)),
        'mla_w_uk': nrm((DEPTH, MLA_KV_RANK, MLA_HEADS, MLA_D_NOPE), MLA_KV_RANK ** -0.5),
        'mla_w_uv': nrm((DEPTH, MLA_KV_RANK, MLA_HEADS, MLA_D_V), MLA_KV_RANK ** -0.5),
        'pool_w': nrm((DEPTH, len(POOL_WINDOWS), POOL_GROUP, POOL_GROUP), POOL_GROUP ** -0.5),
        'pool_scale': gain((DEPTH, BRANCH_W)),
        'diff_qn_g': gain((DEPTH, DIFF_D)),
        'diff_kn_g': gain((DEPTH, DIFF_D)),
        'diff_lambda': nrm((DEPTH, 4, DIFF_D), 0.1),
        'diff_subln_g': gain((DEPTH, DIFF_DV)),
        'rel_bias': nrm((REL_BUCKETS, DIFF_HEADS), 0.5),
        'w_branch': nrm((DEPTH, N_BRANCH, BRANCH_W, D_MODEL), BRANCH_W ** -0.5),
        'w_out': nrm((DEPTH, D_MODEL, D_MODEL), D_MODEL ** -0.5),
        'norm2_g': gain((DEPTH, D_MODEL)),
        'ffn_w_gate': nrm((DEPTH, D_MODEL, D_FF), D_MODEL ** -0.5),
        'ffn_w_up': nrm((DEPTH, D_MODEL, D_FF), D_MODEL ** -0.5),
        'ffn_w_down': nrm((DEPTH, D_FF, D_MODEL), D_FF ** -0.5),
    }


def reference(x_prompt, x_sample, cache_mla, cache_mla_kscale, cache_diff, cache_sb, state_pool,
              page_table, norm1_g, w_in, mla_cq_g, mla_ckv_g, mla_w_uq, mla_qn_g, mla_kn_g,
              mla_w_uk, mla_w_uv, pool_w, pool_scale, diff_qn_g, diff_kn_g, diff_lambda,
              diff_subln_g, rel_bias, w_branch, w_out, norm2_g, ffn_w_gate, ffn_w_up, ffn_w_down):
    P = dict(norm1_g=norm1_g, w_in=w_in, mla_cq_g=mla_cq_g, mla_ckv_g=mla_ckv_g, mla_w_uq=mla_w_uq,
             mla_qn_g=mla_qn_g, mla_kn_g=mla_kn_g, mla_w_uk=mla_w_uk, mla_w_uv=mla_w_uv,
             pool_w=pool_w, pool_scale=pool_scale, diff_qn_g=diff_qn_g, diff_kn_g=diff_kn_g,
             diff_lambda=diff_lambda, diff_subln_g=diff_subln_g, w_branch=w_branch, w_out=w_out,
             norm2_g=norm2_g, ffn_w_gate=ffn_w_gate, ffn_w_up=ffn_w_up, ffn_w_down=ffn_w_down)

    pos_p = jnp.arange(x_prompt.shape[1])

    def prompt_attend(l, tk, lam):
        def block(ql, qr, dq, sq, qpos):
            return (mla_attend(ql, qr, tk['mla_rows'], tk['kscale'], qpos, pos_p),
                    diff_attend(dq, tk['diff_rows'], lam, rel_bias, qpos, pos_p),
                    sb_attend(sq, tk['sb_rows'], qpos, pos_p))
        return over_query_blocks(block, (tk['q_lat'], tk['q_rope'], tk['diff_q'], tk['sb_q']), pos_p)

    n_past = (PAST_LEN // PAGE_SIZE) * PAGE_SIZE
    qpos_s = n_past + jnp.arange(x_sample.shape[1])
    kpos_s = jnp.arange(n_past + x_sample.shape[1])

    def sample_attend(l, tk, lam):
        def one(xs):
            pt, ql, qr, dq, sq, mrow, ks, drow, srow = xs
            def ctx(cache, new):
                past = cache[l, pt].reshape(n_past, cache.shape[-1])
                return jnp.concatenate([past, new], axis=0)[None]
            o_lat = mla_attend(ql[None], qr[None], ctx(cache_mla, mrow), ctx(cache_mla_kscale, ks), qpos_s, kpos_s)
            o_diff = diff_attend(dq[None], ctx(cache_diff, drow), lam, rel_bias, qpos_s, kpos_s)
            o_sb = sb_attend(sq[None], ctx(cache_sb, srow), qpos_s, kpos_s)
            return (o_lat[0], o_diff[0], o_sb[0])
        return lax.map(one, (page_table, tk['q_lat'], tk['q_rope'], tk['diff_q'], tk['sb_q'],
                             tk['mla_rows'], tk['kscale'], tk['diff_rows'], tk['sb_rows']))

    y_prompt, p_mla, p_ks, p_diff, p_sb, p_pool = run_layers(
        x_prompt, pos_p, P, prompt_attend,
        lambda l: jnp.zeros((x_prompt.shape[0], POOL_PAD, BRANCH_W), x_prompt.dtype))
    y_sample, s_mla, s_ks, s_diff, s_sb, s_pool = run_layers(
        x_sample, qpos_s, P, sample_attend, lambda l: state_pool[l])
    return (y_prompt, y_sample, p_mla, p_ks, p_diff, p_sb, p_pool, s_mla, s_ks, s_diff, s_sb, s_pool)
```

```python
import functools
import math

import numpy as np
import jax
import jax.numpy as jnp
from jax import lax
from jax.experimental import pallas as pl
from jax.experimental.pallas import tpu as pltpu

F32 = jnp.float32
BF16 = jnp.bfloat16

N_BRANCH = 4
MLA_HEADS = 8
MLA_D_NOPE = 64
MLA_D_ROPE = 32
MLA_D_QK = MLA_D_NOPE + MLA_D_ROPE
MLA_D_V = 64
MLA_KV_RANK = 128
MLA_Q_RANK = 384
MLA_ROW = MLA_KV_RANK + MLA_D_ROPE
ROPE_BASE = 10000.0
POOL_WINDOWS = (2, 4, 8, 16)
POOL_GROUP = 128
POOL_PAD = max(POOL_WINDOWS) - 1
DIFF_HEADS = 4
DIFF_D = 64
DIFF_DV = 128
DIFF_ROW = 2 * DIFF_D + DIFF_DV
SB_HEADS = 8
SB_D = 64
SB_ROW = 2 * SB_D
REL_BUCKETS = 32
REL_MAX_DIST = 128
EPS = 1e-6
PAGE = 128

LANE = 128
MLA_QW = 256
VMEM_LIMIT = 56 * 1024 * 1024

C_CQ, C_CKV, C_POOL, C_DQ, C_SQ = 0, 384, 512, 1024, 1536
C_KR, C_DK, C_DV, C_SKV, C_END = 2048, 2176, 2304, 2432, 2560


def _dot(a, b):
    return jnp.dot(a, b, preferred_element_type=F32)


def _dot_nt(a, b):
    return lax.dot_general(a, b, (((1,), (1,)), ((), ())), preferred_element_type=F32)


def _dot_hilo(x, m):
    hi = x.astype(BF16)
    lo = (x - hi.astype(F32)).astype(BF16)
    return _dot(hi, m) + _dot(lo, m)


def _rms(x, g):
    return x * lax.rsqrt(jnp.mean(x * x, axis=-1, keepdims=True) + EPS) * g


def _pick(n, pref):
    t = min(n, pref)
    while n % t:
        t //= 2
    return t


def _cparams(sem):
    return pltpu.CompilerParams(dimension_semantics=sem, vmem_limit_bytes=VMEM_LIMIT)


def _inproj_kernel(x_ref, g_ref, w_ref, o_ref, xn_sc):
    @pl.when(pl.program_id(1) == 0)
    def _():
        xn_sc[...] = _rms(x_ref[...], g_ref[...]).astype(BF16)

    o_ref[...] = _dot(xn_sc[...], w_ref[...])


def _inproj(h, g, w):
    n, d = h.shape
    nout = w.shape[1]
    tm = _pick(n, 512)
    tn = _pick(nout, 512)
    return pl.pallas_call(
        _inproj_kernel,
        out_shape=jax.ShapeDtypeStruct((n, nout), F32),
        grid=(n // tm, nout // tn),
        in_specs=[pl.BlockSpec((tm, d), lambda i, j: (i, 0)),
                  pl.BlockSpec((1, d), lambda i, j: (0, 0)),
                  pl.BlockSpec((d, tn), lambda i, j: (0, j))],
        out_specs=pl.BlockSpec((tm, tn), lambda i, j: (i, j)),
        scratch_shapes=[pltpu.VMEM((tm, d), BF16)],
        compiler_params=_cparams(("arbitrary", "arbitrary")),
        name="inproj",
    )(h, g, w)


def _tok_kernel(p_ref, cos_ref, sin_ref, cqg, ckvg, wuq, qng, bd96, kng, knr, mq, wuk, s8,
                dqg, bd64, dkg,
                qmla_ref, qdiff_ref, qsb_ref, mrow_ref, ks_ref, drow_ref,
                kmla_ref, kdiff_ref, ksb_ref):
    tm = p_ref.shape[0]
    lane = lax.broadcasted_iota(jnp.int32, (tm, LANE), 1)
    cos = cos_ref[...]
    sin = sin_ref[...]
    qdt = qmla_ref.dtype

    cqn = _rms(p_ref[:, C_CQ:C_CKV], cqg[...]).astype(BF16)
    q = _dot(cqn, wuq[...])
    ss = _dot_hilo(q * q, bd96[...])
    qn = q * lax.rsqrt(ss * (1.0 / MLA_D_QK) + EPS) * qng[...]
    qnope = qn[:, :512] * kng[...]
    x1 = qn[:, 512:640]
    x2 = qn[:, 640:768]
    o1 = x1 * cos - x2 * sin
    o2 = x1 * sin + x2 * cos
    qall = jnp.concatenate([qnope, o1, o2], axis=1).astype(BF16)
    qcat = _dot(qall, mq[...])
    for h in range(MLA_HEADS):
        qmla_ref[0, h] = qcat[:, h * MLA_QW:(h + 1) * MLA_QW].astype(qdt)

    ckvn = _rms(p_ref[:, C_CKV:C_POOL], ckvg[...])
    knope = _dot(ckvn.astype(BF16), wuk[...])
    ssq = _dot_hilo(knope * knope, s8[...])
    kr = p_ref[:, C_KR:C_DK]
    krsq = jnp.sum(kr * kr, axis=-1, keepdims=True)
    kscale = lax.rsqrt((ssq + krsq) * (1.0 / MLA_D_QK) + EPS)
    ks_ref[...] = kscale[:, :MLA_HEADS]
    krk = kr * knr[...]
    half = MLA_D_ROPE // 2
    rot = jnp.where(lane < half, -pltpu.roll(krk, LANE - half, 1), pltpu.roll(krk, half, 1))
    krrot = jnp.where(lane < MLA_D_ROPE, krk * cos + rot * sin, 0.0)
    mrow_ref[:, :MLA_KV_RANK] = ckvn
    mrow_ref[:, MLA_KV_RANK:] = krrot[:, :MLA_D_ROPE]
    kmla_ref[:, :LANE] = ckvn.astype(kmla_ref.dtype)
    kmla_ref[:, LANE:] = krrot.astype(kmla_ref.dtype)

    dq = p_ref[:, C_DQ:C_SQ]
    ssd = _dot_hilo(dq * dq, bd64[...])
    dqn = dq * lax.rsqrt(ssd * (1.0 / DIFF_D) + EPS) * dqg[...]
    lo_half = lane < DIFF_D
    for h in range(DIFF_HEADS):
        slab = dqn[:, h * LANE:(h + 1) * LANE]
        qdiff_ref[0, 2 * h] = jnp.where(lo_half, slab, 0.0).astype(qdt)
        qdiff_ref[0, 2 * h + 1] = jnp.where(lo_half, 0.0, slab).astype(qdt)
    dk = p_ref[:, C_DK:C_DV]
    dk2 = dk * dk
    s_lo = jnp.sum(jnp.where(lo_half, dk2, 0.0), axis=-1, keepdims=True)
    s_hi = jnp.sum(jnp.where(lo_half, 0.0, dk2), axis=-1, keepdims=True)
    dkn = dk * lax.rsqrt(jnp.where(lo_half, s_lo, s_hi) * (1.0 / DIFF_D) + EPS) * dkg[...]
    dv = p_ref[:, C_DV:C_SKV]
    drow_ref[:, :LANE] = dkn
    drow_ref[:, LANE:] = dv
    kdiff_ref[:, :LANE] = dkn.astype(kdiff_ref.dtype)
    kdiff_ref[:, LANE:] = dv.astype(kdiff_ref.dtype)

    for hh in range(SB_HEADS // 2):
        slab = p_ref[:, C_SQ + hh * LANE:C_SQ + (hh + 1) * LANE]
        qsb_ref[0, 2 * hh] = jnp.where(lo_half, slab, 0.0).astype(qdt)
        qsb_ref[0, 2 * hh + 1] = jnp.where(lo_half, pltpu.roll(slab, SB_D, 1), 0.0).astype(qdt)
    ksb_ref[...] = p_ref[:, C_SKV:C_END].astype(ksb_ref.dtype)


def _tok(proj, row0, groups, tlen, cos, sin, consts, qdt):
    n = groups * tlen
    tm = _pick(tlen, 256)
    nt = tlen // tm
    rb0 = row0 // tm
    assert row0 % tm == 0
    full = lambda a: pl.BlockSpec(a.shape, lambda b, i: (0,) * a.ndim)
    rows = lambda w: pl.BlockSpec((tm, w), lambda b, i: (b * nt + i, 0))
    qspec = lambda w: pl.BlockSpec((1, 8, tm, w), lambda b, i: (b, 0, i, 0))
    out_shape = [
        jax.ShapeDtypeStruct((groups, 8, tlen, MLA_QW), qdt),
        jax.ShapeDtypeStruct((groups, 8, tlen, LANE), qdt),
        jax.ShapeDtypeStruct((groups, 8, tlen, LANE), qdt),
        jax.ShapeDtypeStruct((n, MLA_ROW), F32),
        jax.ShapeDtypeStruct((n, MLA_HEADS), F32),
        jax.ShapeDtypeStruct((n, DIFF_ROW), F32),
        jax.ShapeDtypeStruct((n, MLA_QW), qdt),
        jax.ShapeDtypeStruct((n, DIFF_ROW), qdt),
        jax.ShapeDtypeStruct((n, SB_ROW), qdt),
    ]
    out_specs = [qspec(MLA_QW), qspec(LANE), qspec(LANE), rows(MLA_ROW), rows(MLA_HEADS),
                 rows(DIFF_ROW), rows(MLA_QW), rows(DIFF_ROW), rows(SB_ROW)]
    in_specs = [pl.BlockSpec((tm, C_END), lambda b, i: (rb0 + b * nt + i, 0)),
                pl.BlockSpec((tm, LANE), lambda b, i: (i, 0)),
                pl.BlockSpec((tm, LANE), lambda b, i: (i, 0))] + [full(c) for c in consts]
    return pl.pallas_call(
        _tok_kernel,
        out_shape=out_shape,
        grid=(groups, nt),
        in_specs=in_specs,
        out_specs=out_specs,
        compiler_params=_cparams(("arbitrary", "arbitrary")),
        name="tok",
    )(proj, cos, sin, *consts)


def _pool_prompt_kernel(u_ref, halo_ref, w_ref, sc_ref, o_ref, ext_sc):
    tm = u_ref.shape[0]
    ti = pl.program_id(1)
    u = u_ref[...]
    ext_sc[16:, :] = u
    ext_sc[:16, :] = jnp.where(ti == 0, 0.0, halo_ref[...])
    pos = ti * tm + lax.broadcasted_iota(jnp.int32, (tm, 1), 0)
    outs = []
    for g, w in enumerate(POOL_WINDOWS):
        c0 = g * POOL_GROUP
        wsum = u[:, c0:c0 + POOL_GROUP]
        for d in range(1, w):
            wsum = wsum + ext_sc[16 - d:16 - d + tm, c0:c0 + POOL_GROUP]
        cnt = jnp.minimum(pos + 1, w).astype(F32)
        pooled = wsum / cnt - u[:, c0:c0 + POOL_GROUP]
        outs.append(_dot(pooled.astype(BF16), w_ref[g]))
    o_ref[...] = (jnp.concatenate(outs, axis=1) * sc_ref[...]).astype(o_ref.dtype)


def _pool_prompt(proj, b, t, w, scale):
    tm = _pick(t, 512)
    nt = t // tm
    cb = C_POOL // 512
    return pl.pallas_call(
        _pool_prompt_kernel,
        out_shape=jax.ShapeDtypeStruct((b * t, 512), BF16),
        grid=(b, nt),
        in_specs=[pl.BlockSpec((tm, 512), lambda bi, i: (bi * nt + i, cb)),
                  pl.BlockSpec((16, 512), lambda bi, i: (jnp.maximum((bi * nt + i) * (tm // 16) - 1, 0), cb)),
                  pl.BlockSpec(w.shape, lambda bi, i: (0, 0, 0)),
                  pl.BlockSpec(scale.shape, lambda bi, i: (0, 0))],
        out_specs=pl.BlockSpec((tm, 512), lambda bi, i: (bi * nt + i, 0)),
        scratch_shapes=[pltpu.VMEM((tm + 16, 512), F32)],
        compiler_params=_cparams(("arbitrary", "arbitrary")),
        name="pool_prompt",
    )(proj, proj, w, scale)


def _pool_sample_kernel(ext_ref, w_ref, sc_ref, o_ref, *, n_past):
    ts = o_ref.shape[0]
    for t in range(ts):
        outs = []
        for g, w in enumerate(POOL_WINDOWS):
            c0 = g * POOL_GROUP
            cur = ext_ref[POOL_PAD + t, :, c0:c0 + POOL_GROUP]
            wsum = cur
            for d in range(1, w):
                wsum = wsum + ext_ref[POOL_PAD + t - d, :, c0:c0 + POOL_GROUP]
            cnt = float(min(n_past + t + 1, w))
            pooled = wsum / cnt - cur
            outs.append(_dot(pooled.astype(BF16), w_ref[g]))
        o_ref[t] = jnp.concatenate(outs, axis=1) * sc_ref[...]


def _pool_sample(ext, w, scale, n_past):
    rows, seqs, _ = ext.shape
    ts = rows - POOL_PAD
    return pl.pallas_call(
        functools.partial(_pool_sample_kernel, n_past=n_past),
        out_shape=jax.ShapeDtypeStruct((ts, seqs, 512), F32),
        grid=(1,),
        in_specs=[pl.BlockSpec(ext.shape, lambda i: (0, 0, 0)),
                  pl.BlockSpec(w.shape, lambda i: (0, 0, 0)),
                  pl.BlockSpec(scale.shape, lambda i: (0, 0))],
        out_specs=pl.BlockSpec((ts, seqs, 512), lambda i: (0, 0, 0)),
        compiler_params=_cparams(("arbitrary",)),
        name="pool_sample",
    )(ext, w, scale)


def _softmax_step(s, v, m_sc, l_sc, acc_sc):
    m_old = m_sc[...]
    m_new = jnp.maximum(m_old, jnp.max(s, axis=-1, keepdims=True))
    alpha = jnp.exp(m_old - m_new)
    p = jnp.exp(s - m_new)
    l_sc[...] = alpha * l_sc[...] + jnp.sum(p, axis=-1, keepdims=True)
    acc_sc[...] = alpha * acc_sc[...] + _dot(p.astype(BF16), v)
    m_sc[...] = m_new


def _mla_prompt_kernel(q_ref, k_ref, ks_ref, wuv_ref, o_ref, m_sc, l_sc, acc_sc, *, tq, tk):
    qi = pl.program_id(1)
    q = q_ref[0].reshape(MLA_HEADS * tq, MLA_QW)
    m_sc[...] = jnp.full(m_sc.shape, -jnp.inf, F32)
    l_sc[...] = jnp.zeros(l_sc.shape, F32)
    acc_sc[...] = jnp.zeros(acc_sc.shape, F32)
    nkb = ((qi + 1) * tq + tk - 1) // tk
    qpos = qi * tq + lax.broadcasted_iota(jnp.int32, (1, tq, tk), 1)
    kiota = lax.broadcasted_iota(jnp.int32, (1, tq, tk), 2)

    def body(kb, carry):
        k = k_ref[0, pl.ds(pl.multiple_of(kb * tk, tk), tk), :]
        s = _dot_nt(q, k).reshape(MLA_HEADS, tq, tk)
        ks = ks_ref[0, kb] * (MLA_D_QK ** -0.5)
        s = s * ks[:, None, :]
        s = jnp.where(kb * tk + kiota <= qpos, s, -jnp.inf).reshape(MLA_HEADS * tq, tk)
        _softmax_step(s, k[:, :MLA_KV_RANK], m_sc, l_sc, acc_sc)
        return carry

    lax.fori_loop(0, nkb, body, 0)
    o = acc_sc[...] / l_sc[...]
    ocat = jnp.concatenate([o[h * tq:(h + 1) * tq] for h in range(MLA_HEADS)], axis=1)
    o_ref[...] = _dot(ocat.astype(BF16), wuv_ref[...]).astype(o_ref.dtype)


def _diff_prompt_kernel(lam_ref, q_ref, k_ref, bias_ref, g_ref, o_ref, m_sc, l_sc, acc_sc,
                        *, tq, tk, lam_init):
    qi = pl.program_id(1)
    nh = 2 * DIFF_HEADS
    q = q_ref[0].reshape(nh * tq, LANE)
    m_sc[...] = jnp.full(m_sc.shape, -jnp.inf, F32)
    l_sc[...] = jnp.zeros(l_sc.shape, F32)
    acc_sc[...] = jnp.zeros(acc_sc.shape, F32)
    nkb = ((qi + 1) * tq + tk - 1) // tk
    ncase = bias_ref.shape[0]
    qpos = qi * tq + lax.broadcasted_iota(jnp.int32, (1, 1, tq, tk), 2)
    kiota = lax.broadcasted_iota(jnp.int32, (1, 1, tq, tk), 3)

    def body(kb, carry):
        k = k_ref[0, pl.ds(pl.multiple_of(kb * tk, tk), tk), :]
        s = _dot_nt(q, k[:, :LANE]).reshape(DIFF_HEADS, 2, tq, tk)
        case = jnp.minimum((qi * tq - kb * tk) // tq, ncase - 1)
        s = s * (DIFF_D ** -0.5) + bias_ref[case][:, None, :, :]
        s = jnp.where(kb * tk + kiota <= qpos, s, -jnp.inf).reshape(nh * tq, tk)
        _softmax_step(s, k[:, LANE:], m_sc, l_sc, acc_sc)
        return carry

    lax.fori_loop(0, nkb, body, 0)
    o = acc_sc[...] / l_sc[...]
    lam = lam_ref[0]
    outs = []
    for h in range(DIFF_HEADS):
        oh = o[(2 * h) * tq:(2 * h + 1) * tq] - lam * o[(2 * h + 1) * tq:(2 * h + 2) * tq]
        outs.append(_rms(oh, g_ref[...]) * (1.0 - lam_init))
    o_ref[...] = jnp.concatenate(outs, axis=1).astype(o_ref.dtype)


def _softplus(z):
    return jnp.maximum(z, 0.0) + jnp.log1p(jnp.exp(-jnp.abs(z)))


def _sb_block(z, mask, v, umat, r):
    sp = _softplus(z)
    lk = -sp
    if mask is not None:
        shape3 = (-1,) + mask.shape[1:]
        lk = jnp.where(mask, lk.reshape(shape3), 0.0).reshape(z.shape)
    suf = _dot_hilo(lk, umat)
    a = jnp.exp((z - sp) + suf + r)
    if mask is not None:
        a = jnp.where(mask, a.reshape(shape3), 0.0).reshape(z.shape)
    return _dot(a.astype(BF16), v), r + jnp.sum(lk, axis=-1, keepdims=True)


def _sb_prompt_kernel(q_ref, k_ref, u_ref, sel_ref, o_ref, r_sc, acc_sc, *, tq, tk):
    qi = pl.program_id(1)
    q = q_ref[0].reshape(SB_HEADS * tq, LANE)
    r_sc[...] = jnp.zeros(r_sc.shape, F32)
    acc_sc[...] = jnp.zeros(acc_sc.shape, F32)
    nkb = ((qi + 1) * tq + tk - 1) // tk
    qpos = qi * tq + lax.broadcasted_iota(jnp.int32, (1, tq, tk), 1)
    kiota = lax.broadcasted_iota(jnp.int32, (1, tq, tk), 2)

    def body(it, carry):
        kb = nkb - 1 - it
        k = k_ref[0, pl.ds(pl.multiple_of(kb * tk, tk), tk), :]
        z = _dot_nt(q, k) * (SB_D ** -0.5)
        mask = kb * tk + kiota < qpos
        pv, r_new = _sb_block(z, mask, k, u_ref[...], r_sc[...])
        acc_sc[...] += pv
        r_sc[...] = r_new
        return carry

    lax.fori_loop(0, nkb, body, 0)
    o = acc_sc[...]
    ocat = jnp.concatenate([o[h * tq:(h + 1) * tq] for h in range(SB_HEADS)], axis=1)
    o_ref[...] = _dot(ocat.astype(BF16), sel_ref[...]).astype(o_ref.dtype)


def _prompt_attn(kind, q, kmat, extra, b, t, tq, tk, **kw):
    nq = t // tq
    w = q.shape[-1]
    kw_ = kmat.shape[-1]
    k3 = kmat.reshape(b, t, kw_)
    qspec = pl.BlockSpec((1, 8, tq, w), lambda bi, i, *_: (bi, 0, i, 0))
    kspec = pl.BlockSpec((1, t, kw_), lambda bi, i, *_: (bi, 0, 0))
    ospec = pl.BlockSpec((tq, 512), lambda bi, i, *_: (bi * nq + i, 0))
    full = lambda a: pl.BlockSpec(a.shape, lambda bi, i, *_: (0,) * a.ndim)
    rows = 8 * tq
    out_shape = jax.ShapeDtypeStruct((b * t, 512), BF16)
    if kind == "mla":
        ks, wuv = extra
        return pl.pallas_call(
            functools.partial(_mla_prompt_kernel, tq=tq, tk=tk),
            out_shape=out_shape, grid=(b, nq),
            in_specs=[qspec, kspec,
                      pl.BlockSpec((1,) + ks.shape[1:], lambda bi, i: (bi, 0, 0, 0)), full(wuv)],
            out_specs=ospec,
            scratch_shapes=[pltpu.VMEM((rows, 1), F32), pltpu.VMEM((rows, 1), F32),
                            pltpu.VMEM((rows, LANE), F32)],
            compiler_params=_cparams(("arbitrary", "arbitrary")), name="mla_prompt",
        )(q, k3, ks, wuv)
    if kind == "diff":
        lam, bias, g = extra
        return pl.pallas_call(
            functools.partial(_diff_prompt_kernel, tq=tq, tk=tk, lam_init=kw["lam_init"]),
            out_shape=out_shape,
            grid_spec=pltpu.PrefetchScalarGridSpec(
                num_scalar_prefetch=1, grid=(b, nq),
                in_specs=[qspec, kspec, full(bias), full(g)],
                out_specs=ospec,
                scratch_shapes=[pltpu.VMEM((rows, 1), F32), pltpu.VMEM((rows, 1), F32),
                                pltpu.VMEM((rows, LANE), F32)]),
            compiler_params=_cparams(("arbitrary", "arbitrary")), name="diff_prompt",
        )(lam, q, k3, bias, g)
    umat, sel = extra
    return pl.pallas_call(
        functools.partial(_sb_prompt_kernel, tq=tq, tk=tk),
        out_shape=out_shape, grid=(b, nq),
        in_specs=[qspec, kspec, full(umat), full(sel)],
        out_specs=ospec,
        scratch_shapes=[pltpu.VMEM((rows, 1), F32), pltpu.VMEM((rows, LANE), F32)],
        compiler_params=_cparams(("arbitrary", "arbitrary")), name="sb_prompt",
    )(q, k3, umat, sel)


def _page_copies(cache, layer, pt_ref, seq, chunk, buf, sem, slot, npg):
    return [pltpu.make_async_copy(cache.at[layer, pt_ref[seq, chunk * npg + i]],
                                  buf.at[slot, i], sem.at[slot]) for i in range(npg)]


def _paged_loop(caches, bufs, sems, layer, pt_ref, nseq, nch, npg, compute, reverse):
    seq = pl.program_id(0)

    def chunk_of(c):
        return nch - 1 - c if reverse else c

    def start(sq, c, slot):
        for cache, buf, sem in zip(caches, bufs, sems):
            for cp in _page_copies(cache, layer, pt_ref, sq, chunk_of(c), buf, sem, slot, npg):
                cp.start()

    def wait(sq, c, slot):
        for cache, buf, sem in zip(caches, bufs, sems):
            for cp in _page_copies(cache, layer, pt_ref, sq, chunk_of(c), buf, sem, slot, npg):
                cp.wait()

    @pl.when(seq == 0)
    def _():
        start(seq, 0, 0)

    def body(c, carry):
        slot = (seq * nch + c) % 2
        last = c == nch - 1
        nseq_ = jnp.where(last, seq + 1, seq)
        nc = jnp.where(last, 0, c + 1)

        @pl.when(nseq_ < nseq)
        def _():
            start(nseq_, nc, 1 - slot)

        wait(seq, c, slot)
        return compute(chunk_of(c), slot, carry)

    return body


def _mla_sample_kernel(pt_ref, q_ref, knew_ref, ksnew_ref, wuv_ref, rows_hbm, ks_hbm, o_ref,
                       rbuf, kbuf, rsem, ksem, *, layer, nseq, nch, npg):
    ts = q_ref.shape[1]
    nr = MLA_HEADS * ts
    tk = npg * PAGE
    q = q_ref[...].reshape(nr, MLA_QW).astype(BF16)
    qc = q[:, :MLA_KV_RANK]
    qr = q[:, MLA_KV_RANK:]
    scale = MLA_D_QK ** -0.5

    def update(s, v, carry):
        m, l, acc = carry
        m_new = jnp.maximum(m, jnp.max(s, axis=-1, keepdims=True))
        alpha = jnp.exp(m - m_new)
        p = jnp.exp(s - m_new)
        return (m_new, alpha * l + jnp.sum(p, axis=-1, keepdims=True),
                alpha * acc + _dot(p.astype(BF16), v))

    def compute(chunk, slot, carry):
        rows = rbuf[slot].reshape(tk, MLA_ROW).astype(BF16)
        c = rows[:, :MLA_KV_RANK]
        krp = jnp.concatenate([rows[:, MLA_KV_RANK:], jnp.zeros((tk, LANE - MLA_D_ROPE), BF16)], axis=1)
        s = (_dot_nt(qc, c) + _dot_nt(qr, krp)).reshape(MLA_HEADS, ts, tk)
        ks = jnp.concatenate([kbuf[slot, i] for i in range(npg)], axis=1) * scale
        s = (s * ks[:, None, :]).reshape(nr, tk)
        return update(s, c, carry)

    body = _paged_loop((rows_hbm, ks_hbm), (rbuf, kbuf), (rsem, ksem), layer, pt_ref, nseq, nch, npg,
                       compute, reverse=False)
    init = (jnp.full((nr, 1), -jnp.inf, F32), jnp.zeros((nr, 1), F32), jnp.zeros((nr, MLA_KV_RANK), F32))
    carry = lax.fori_loop(0, nch, body, init)

    kn = knew_ref[0]
    s = _dot_nt(q, kn).reshape(MLA_HEADS, ts, PAGE)
    s = s * (ksnew_ref[0] * scale)[:, None, :]
    t_i = lax.broadcasted_iota(jnp.int32, (1, ts, PAGE), 1)
    j_i = lax.broadcasted_iota(jnp.int32, (1, ts, PAGE), 2)
    s = jnp.where(j_i <= t_i, s, -jnp.inf).reshape(nr, PAGE)
    m, l, acc = update(s, kn[:, :MLA_KV_RANK], carry)
    o = acc / l
    ocat = jnp.concatenate([o[h * ts:(h + 1) * ts] for h in range(MLA_HEADS)], axis=1)
    o_ref[...] = _dot(ocat.astype(BF16), wuv_ref[...])


def _diff_sample_kernel(pt_ref, lam_ref, q_ref, knew_ref, bias_ref, bnew_ref, g_ref, rows_hbm, o_ref,
                        rbuf, rsem, *, layer, nseq, nch, npg, lam_init):
    ts = q_ref.shape[1]
    nh = 2 * DIFF_HEADS
    nr = nh * ts
    tk = npg * PAGE
    q = q_ref[...].reshape(nr, LANE).astype(BF16)

    def update(s, v, carry):
        m, l, acc = carry
        m_new = jnp.maximum(m, jnp.max(s, axis=-1, keepdims=True))
        alpha = jnp.exp(m - m_new)
        p = jnp.exp(s - m_new)
        return (m_new, alpha * l + jnp.sum(p, axis=-1, keepdims=True),
                alpha * acc + _dot(p.astype(BF16), v))

    def compute(chunk, slot, carry):
        rows = rbuf[slot].reshape(tk, DIFF_ROW).astype(BF16)
        s = _dot_nt(q, rows[:, :LANE]).reshape(DIFF_HEADS, 2, ts, tk) * (DIFF_D ** -0.5)
        case = jnp.where(chunk == nch - 1, 1, 0)
        s = (s + bias_ref[case][:, None, :, :]).reshape(nr, tk)
        return update(s, rows[:, LANE:], carry)

    body = _paged_loop((rows_hbm,), (rbuf,), (rsem,), layer, pt_ref, nseq, nch, npg, compute, reverse=False)
    init = (jnp.full((nr, 1), -jnp.inf, F32), jnp.zeros((nr, 1), F32), jnp.zeros((nr, DIFF_DV), F32))
    carry = lax.fori_loop(0, nch, body, init)

    kn = knew_ref[0]
    s = _dot_nt(q, kn[:, :LANE]).reshape(DIFF_HEADS, 2, ts, PAGE) * (DIFF_D ** -0.5)
    s = s + bnew_ref[...][:, None, :, :]
    t_i = lax.broadcasted_iota(jnp.int32, (1, 1, ts, PAGE), 2)
    j_i = lax.broadcasted_iota(jnp.int32, (1, 1, ts, PAGE), 3)
    s = jnp.where(j_i <= t_i, s, -jnp.inf).reshape(nr, PAGE)
    m, l, acc = update(s, kn[:, LANE:], carry)
    o = acc / l
    lam = lam_ref[0]
    outs = []
    for h in range(DIFF_HEADS):
        oh = o[(2 * h) * ts:(2 * h + 1) * ts] - lam * o[(2 * h + 1) * ts:(2 * h + 2) * ts]
        outs.append(_rms(oh, g_ref[...]) * (1.0 - lam_init))
    o_ref[...] = jnp.concatenate(outs, axis=1)


def _sb_sample_kernel(pt_ref, q_ref, knew_ref, u_ref, sel_ref, rows_hbm, o_ref, rbuf, rsem,
                      *, layer, nseq, nch, npg):
    ts = q_ref.shape[1]
    nr = SB_HEADS * ts
    q = q_ref[...].reshape(nr, LANE).astype(BF16)
    umat = u_ref[...]

    kn = knew_ref[0]
    z = _dot_nt(q, kn) * (SB_D ** -0.5)
    t_i = lax.broadcasted_iota(jnp.int32, (1, ts, PAGE), 1)
    j_i = lax.broadcasted_iota(jnp.int32, (1, ts, PAGE), 2)
    mask = j_i < t_i
    acc0, r0 = _sb_block(z, mask, kn, umat, jnp.zeros((nr, 1), F32))

    def compute(chunk, slot, carry):
        acc, r = carry
        for i in reversed(range(npg)):
            k = rbuf[slot, i].astype(BF16)
            z = _dot_nt(q, k) * (SB_D ** -0.5)
            pv, r = _sb_block(z, None, k, umat, r)
            acc = acc + pv
        return acc, r

    body = _paged_loop((rows_hbm,), (rbuf,), (rsem,), layer, pt_ref, nseq, nch, npg, compute, reverse=True)
    acc, _ = lax.fori_loop(0, nch, body, (acc0, r0))
    ocat = jnp.concatenate([acc[h * ts:(h + 1) * ts] for h in range(SB_HEADS)], axis=1)
    o_ref[...] = _dot(ocat.astype(BF16), sel_ref[...])


def _sample_attn(kind, layer, pt, q, knew, extra, caches, npg):
    nseq, npages = pt.shape
    nch = npages // npg
    ts = q.shape[1] // nseq
    w = q.shape[-1]
    kw_ = knew.shape[-1]
    qspec = pl.BlockSpec((8, ts, w), lambda s, *_: (0, s, 0))
    knspec = pl.BlockSpec((1, PAGE, kw_), lambda s, *_: (s, 0, 0))
    ospec = pl.BlockSpec((ts, 512), lambda s, *_: (s, 0))
    full = lambda a: pl.BlockSpec(a.shape, lambda s, *_: (0,) * a.ndim)
    anyspec = pl.BlockSpec(memory_space=pl.ANY)
    out_shape = jax.ShapeDtypeStruct((nseq * ts, 512), F32)
    common = dict(layer=layer, nseq=nseq, nch=nch, npg=npg)
    if kind == "mla":
        ksnew, wuv = extra
        rows_c, ks_c = caches
        gs = pltpu.PrefetchScalarGridSpec(
            num_scalar_prefetch=1, grid=(nseq,),
            in_specs=[qspec, knspec, pl.BlockSpec((1, 8, PAGE), lambda s, *_: (s, 0, 0)), full(wuv),
                      anyspec, anyspec],
            out_specs=ospec,
            scratch_shapes=[pltpu.VMEM((2, npg, PAGE, MLA_ROW), F32), pltpu.VMEM((2, npg, 8, PAGE), F32),
                            pltpu.SemaphoreType.DMA((2,)), pltpu.SemaphoreType.DMA((2,))])
        return pl.pallas_call(functools.partial(_mla_sample_kernel, **common), out_shape=out_shape,
                              grid_spec=gs, compiler_params=_cparams(("arbitrary",)),
                              name="mla_sample")(pt, q, knew, ksnew, wuv, rows_c, ks_c)
    if kind == "diff":
        lam, bias, bnew, g, lam_init = extra
        (rows_c,) = caches
        gs = pltpu.PrefetchScalarGridSpec(
            num_scalar_prefetch=2, grid=(nseq,),
            in_specs=[qspec, knspec, full(bias), full(bnew), full(g), anyspec],
            out_specs=ospec,
            scratch_shapes=[pltpu.VMEM((2, npg, PAGE, DIFF_ROW), F32), pltpu.SemaphoreType.DMA((2,))])
        return pl.pallas_call(functools.partial(_diff_sample_kernel, lam_init=lam_init, **common),
                              out_shape=out_shape, grid_spec=gs, compiler_params=_cparams(("arbitrary",)),
                              name="diff_sample")(pt, lam, q, knew, bias, bnew, g, rows_c)
    umat, sel = extra
    (rows_c,) = caches
    gs = pltpu.PrefetchScalarGridSpec(
        num_scalar_prefetch=1, grid=(nseq,),
        in_specs=[qspec, knspec, full(umat), full(sel), anyspec],
        out_specs=ospec,
        scratch_shapes=[pltpu.VMEM((2, npg, PAGE, SB_ROW), F32), pltpu.SemaphoreType.DMA((2,))])
    return pl.pallas_call(functools.partial(_sb_sample_kernel, **common), out_shape=out_shape,
                          grid_spec=gs, compiler_params=_cparams(("arbitrary",)),
                          name="sb_sample")(pt, q, knew, umat, sel, rows_c)


def _merge_kernel(h_ref, g_ref, wg0, wg1, wg2, wg3, o_ref, wb_ref, out_ref, xn_sc):
    @pl.when(pl.program_id(1) == 0)
    def _():
        xn_sc[...] = _rms(h_ref[...], g_ref[...]).astype(BF16)

    xn = xn_sc[...]
    acc = None
    for g, wg in enumerate((wg0, wg1, wg2, wg3)):
        gate = jax.nn.sigmoid(_dot(xn, wg[...]))
        term = gate * _dot(o_ref[g], wb_ref[g])
        acc = term if acc is None else acc + term
    out_ref[...] = acc.astype(out_ref.dtype)


def _merge(h, g, wgates, o_all, wb):
    n, d = h.shape
    tm = _pick(n, 512)
    tn = _pick(d, 512)
    nj = d // tn
    bw = o_all.shape[-1]
    gspec = lambda gi: pl.BlockSpec((d, tn), lambda i, j: (0, gi * nj + j))
    return pl.pallas_call(
        _merge_kernel,
        out_shape=jax.ShapeDtypeStruct((n, d), BF16),
        grid=(n // tm, nj),
        in_specs=[pl.BlockSpec((tm, d), lambda i, j: (i, 0)),
                  pl.BlockSpec((1, d), lambda i, j: (0, 0)),
                  gspec(0), gspec(1), gspec(2), gspec(3),
                  pl.BlockSpec((N_BRANCH, tm, bw), lambda i, j: (0, i, 0)),
                  pl.BlockSpec((N_BRANCH, bw, tn), lambda i, j: (0, 0, j))],
        out_specs=pl.BlockSpec((tm, tn), lambda i, j: (i, j)),
        scratch_shapes=[pltpu.VMEM((tm, d), BF16)],
        compiler_params=_cparams(("arbitrary", "arbitrary")),
        name="merge",
    )(h, g, wgates, wgates, wgates, wgates, o_all, wb)


def _outproj_kernel(m_ref, w_ref, h_ref, o_ref):
    o_ref[...] = h_ref[...] + _dot(m_ref[...], w_ref[...])


def _outproj(merged, w, h):
    n, d = h.shape
    tm = _pick(n, 512)
    tn = _pick(d, 1024)
    return pl.pallas_call(
        _outproj_kernel,
        out_shape=jax.ShapeDtypeStruct((n, d), F32),
        grid=(n // tm, d // tn),
        in_specs=[pl.BlockSpec((tm, d), lambda i, j: (i, 0)),
                  pl.BlockSpec((d, tn), lambda i, j: (0, j)),
                  pl.BlockSpec((tm, tn), lambda i, j: (i, j))],
        out_specs=pl.BlockSpec((tm, tn), lambda i, j: (i, j)),
        compiler_params=_cparams(("arbitrary", "arbitrary")),
        name="outproj",
    )(merged, w, h)


def _ffn_kernel(h_ref, g_ref, wg_ref, wu_ref, wd_ref, o_ref, hn_sc, acc_sc):
    f = pl.program_id(1)

    @pl.when(f == 0)
    def _():
        h = h_ref[...]
        hn_sc[...] = _rms(h, g_ref[...]).astype(BF16)
        acc_sc[...] = h

    hn = hn_sc[...]
    a = jax.nn.silu(_dot(hn, wg_ref[...])) * _dot(hn, wu_ref[...])
    acc_sc[...] += _dot(a.astype(BF16), wd_ref[...])

    @pl.when(f == pl.num_programs(1) - 1)
    def _():
        o_ref[...] = acc_sc[...]


def _ffn(h, g, wg, wu, wd):
    n, d = h.shape
    dff = wg.shape[1]
    tm = _pick(n, 512)
    tf = 512 if dff % 512 == 0 else _pick(dff, 256)
    return pl.pallas_call(
        _ffn_kernel,
        out_shape=jax.ShapeDtypeStruct((n, d), F32),
        grid=(n // tm, dff // tf),
        in_specs=[pl.BlockSpec((tm, d), lambda i, f: (i, 0)),
                  pl.BlockSpec((1, d), lambda i, f: (0, 0)),
                  pl.BlockSpec((d, tf), lambda i, f: (0, f)),
                  pl.BlockSpec((d, tf), lambda i, f: (0, f)),
                  pl.BlockSpec((tf, d), lambda i, f: (f, 0))],
        out_specs=pl.BlockSpec((tm, d), lambda i, f: (i, 0)),
        scratch_shapes=[pltpu.VMEM((tm, d), BF16), pltpu.VMEM((tm, d), F32)],
        compiler_params=_cparams(("arbitrary", "arbitrary")),
        name="ffn",
    )(h, g, wg, wu, wd)


def _rope_tables(pos):
    half = MLA_D_ROPE // 2
    inv = ROPE_BASE ** (-jnp.arange(half, dtype=F32) / half)
    ang = pos.astype(F32)[:, None] * inv[None, :]
    reps = LANE // half
    return jnp.tile(jnp.cos(ang), (1, reps)), jnp.tile(jnp.sin(ang), (1, reps))


def _bias_by_distance(rel_bias, n):
    exact = REL_BUCKETS // 2
    n = jnp.maximum(n, 0)
    nf = jnp.maximum(n, 1).astype(F32)
    large = exact + (jnp.log(nf / exact) / math.log(REL_MAX_DIST / exact)
                     * (REL_BUCKETS - exact)).astype(jnp.int32)
    bucket = jnp.where(n < exact, n, jnp.minimum(large, REL_BUCKETS - 1))
    return jnp.moveaxis(rel_bias[bucket].astype(F32), -1, 0)


def _static_mats():
    col = np.arange(768)
    head = np.where(col < 512, col // 64, np.where(col < 640, (col - 512) // 16, (col - 640) // 16))
    bd96 = (head[:, None] == head[None, :]).astype(np.float32)
    c5 = np.arange(512)
    bd64 = (c5[:, None] // 64 == c5[None, :] // 64).astype(np.float32)
    s8 = np.zeros((512, LANE), np.float32)
    s8[c5, c5 // 64] = 1.0
    half = MLA_D_ROPE // 2
    selq = np.zeros((256, MLA_HEADS * MLA_QW), np.float32)
    for h in range(MLA_HEADS):
        for i in range(half):
            selq[h * half + i, h * MLA_QW + MLA_KV_RANK + i] = 1.0
            selq[128 + h * half + i, h * MLA_QW + MLA_KV_RANK + half + i] = 1.0
    selsb = np.zeros((SB_HEADS * LANE, 512), np.float32)
    for h in range(SB_HEADS):
        for d in range(SB_D):
            selsb[h * LANE + SB_D + d, h * SB_D + d] = 1.0
    return bd96, bd64, s8, selq, selsb


def _suffix_matrix(tk):
    j = np.arange(tk)
    return jnp.asarray((j[:, None] > j[None, :]).astype(np.float32), BF16)


def _layer_consts(l, P):
    bd96, bd64, s8, selq, selsb = _static_mats()
    row = lambda v: v.reshape(1, -1).astype(F32)
    w_uq = P["mla_w_uq"][l]
    wuq = jnp.concatenate([w_uq[:, :, :MLA_D_NOPE].reshape(MLA_Q_RANK, 512),
                           w_uq[:, :, MLA_D_NOPE:MLA_D_NOPE + 16].reshape(MLA_Q_RANK, 128),
                           w_uq[:, :, MLA_D_NOPE + 16:].reshape(MLA_Q_RANK, 128)], axis=1).astype(BF16)
    qn = P["mla_qn_g"][l]
    qng = row(jnp.concatenate([jnp.tile(qn[:64], 8), jnp.tile(qn[64:80], 8), jnp.tile(qn[80:], 8)]))
    kn = P["mla_kn_g"][l]
    kng = row(jnp.tile(kn[:MLA_D_NOPE], 8))
    knr = row(jnp.concatenate([kn[MLA_D_NOPE:], jnp.zeros((LANE - MLA_D_ROPE,), F32)]))
    w_uk = P["mla_w_uk"][l]
    blk = jnp.pad(jnp.transpose(w_uk, (1, 2, 0)), ((0, 0), (0, 0), (0, MLA_QW - MLA_KV_RANK)))
    m_nope = jnp.einsum("hjc,hg->hjgc", blk, jnp.eye(MLA_HEADS, dtype=F32)).reshape(512, MLA_HEADS * MLA_QW)
    mq = jnp.concatenate([m_nope, jnp.asarray(selq)], axis=0).astype(BF16)
    wuk = w_uk.reshape(MLA_KV_RANK, 512).astype(BF16)
    tok_consts = [row(P["mla_cq_g"][l]), row(P["mla_ckv_g"][l]), wuq, qng, jnp.asarray(bd96, BF16),
                  kng, knr, mq, wuk, jnp.asarray(s8, BF16),
                  row(jnp.tile(P["diff_qn_g"][l], 8)), jnp.asarray(bd64, BF16),
                  row(jnp.tile(P["diff_kn_g"][l], 2))]
    wuv = jnp.einsum("chd,hg->hcgd", P["mla_w_uv"][l], jnp.eye(MLA_HEADS, dtype=F32))
    wuv = wuv.reshape(MLA_HEADS * MLA_KV_RANK, 512).astype(BF16)
    return tok_consts, wuv, jnp.asarray(selsb, BF16)


def _in_weights(w):
    d = w.shape[0]
    small = jnp.concatenate([w[:, 0:512], w[:, 544:1056], w[:, 1056:1568], w[:, 1824:2336],
                             w[:, 512:544], jnp.zeros((d, LANE - MLA_D_ROPE), w.dtype),
                             w[:, 1568:1824], w[:, 2336:2464]], axis=1).astype(BF16)
    return small, w[:, 2464:].astype(BF16)


def kernel(x_prompt, x_sample, cache_mla, cache_mla_kscale, cache_diff, cache_sb, state_pool,
           page_table, norm1_g, w_in, mla_cq_g, mla_ckv_g, mla_w_uq, mla_qn_g, mla_kn_g,
           mla_w_uk, mla_w_uv, pool_w, pool_scale, diff_qn_g, diff_kn_g, diff_lambda,
           diff_subln_g, rel_bias, w_branch, w_out, norm2_g, ffn_w_gate, ffn_w_up, ffn_w_down):
    P = dict(mla_cq_g=mla_cq_g, mla_ckv_g=mla_ckv_g, mla_w_uq=mla_w_uq, mla_qn_g=mla_qn_g,
             mla_kn_g=mla_kn_g, mla_w_uk=mla_w_uk, mla_w_uv=mla_w_uv, diff_qn_g=diff_qn_g,
             diff_kn_g=diff_kn_g)
    b, t, d = x_prompt.shape
    bs, ts, _ = x_sample.shape
    depth = w_in.shape[0]
    npages = page_table.shape[1]
    n_past = npages * PAGE
    n_p, n_s = b * t, bs * ts
    row = lambda v: v.reshape(1, -1).astype(F32)

    tq = _pick(t, 128)
    tk = _pick(t, 256)
    npg = _pick(npages, 16)

    cos_p, sin_p = _rope_tables(jnp.arange(t))
    cos_s, sin_s = _rope_tables(n_past + jnp.arange(ts))
    cos_s, sin_s = jnp.tile(cos_s, (bs, 1)), jnp.tile(sin_s, (bs, 1))

    ncase = -(-(tk + 112) // tq)
    ii = jnp.arange(tq)[:, None]
    jj = jnp.arange(tk)[None, :]
    dist_p = jnp.stack([dcase * tq + ii - jj for dcase in range(ncase)] +
                       [jnp.full((tq, tk), REL_MAX_DIST, jnp.int32)])
    bias_p = jnp.moveaxis(_bias_by_distance(rel_bias, dist_p), 0, 1)
    tkc = npg * PAGE
    tt = jnp.arange(ts)[:, None]
    dist_last = n_past + tt - (n_past - tkc + jnp.arange(tkc)[None, :])
    dist_s = jnp.stack([jnp.full((ts, tkc), REL_MAX_DIST + tkc, jnp.int32), dist_last])
    bias_s = jnp.moveaxis(_bias_by_distance(rel_bias, dist_s), 0, 1)
    bias_new = _bias_by_distance(rel_bias, tt - jnp.arange(PAGE)[None, :])

    ks_cache_t = jnp.swapaxes(cache_mla_kscale, 2, 3)
    umat_p = _suffix_matrix(tk)
    umat_s = _suffix_matrix(PAGE)

    h = jnp.concatenate([x_prompt.reshape(n_p, d), x_sample.reshape(n_s, d)], axis=0)
    outs = {k: [] for k in ("mla_p", "ks_p", "diff_p", "sb_p", "pool_p",
                            "mla_s", "ks_s", "diff_s", "sb_s", "pool_s")}
    for l in range(depth):
        tok_consts, wuv, selsb = _layer_consts(l, P)
        w_small, w_gates = _in_weights(w_in[l])
        lq1, lk1, lq2, lk2 = diff_lambda[l].astype(F32)
        lam_init = 0.8 - 0.6 * math.exp(-0.3 * l)
        lam = (jnp.exp(jnp.sum(lq1 * lk1)) - jnp.exp(jnp.sum(lq2 * lk2)) + lam_init).reshape(1)
        subln = row(diff_subln_g[l])

        proj = _inproj(h, row(norm1_g[l]), w_small)

        (qm_p, qd_p, qs_p, mrow_p, ks_p, drow_p, km_p, kd_p, kb_p) = _tok(
            proj, 0, b, t, cos_p, sin_p, tok_consts, BF16)
        (qm_s, qd_s, qs_s, mrow_s, ks_s, drow_s, km_s, kd_s, kb_s) = _tok(
            proj, n_p, 1, n_s, cos_s, sin_s, tok_consts, F32)

        pw = pool_w[l].astype(BF16)
        psc = row(pool_scale[l])
        o_pool_p = _pool_prompt(proj, b, t, pw, psc)
        u_s = proj[n_p:, C_POOL:C_DQ].reshape(bs, ts, 512)
        ext_s = jnp.concatenate([state_pool[l], u_s], axis=1)
        o_pool_s = _pool_sample(jnp.swapaxes(ext_s, 0, 1), pw, psc, n_past)
        o_pool_s = jnp.swapaxes(o_pool_s, 0, 1).reshape(n_s, 512)

        ks_t = jnp.transpose(ks_p.reshape(b, t // tk, tk, MLA_HEADS), (0, 1, 3, 2))
        o_mla_p = _prompt_attn("mla", qm_p, km_p, (ks_t, wuv), b, t, tq, tk)
        o_diff_p = _prompt_attn("diff", qd_p, kd_p, (lam, bias_p, subln), b, t, tq, tk, lam_init=lam_init)
        o_sb_p = _prompt_attn("sb", qs_p, kb_p, (umat_p, selsb), b, t, tq, tk)

        pad_new = lambda a: jnp.pad(a.reshape(bs, ts, -1), ((0, 0), (0, PAGE - ts), (0, 0))).astype(BF16)
        ksnew = jnp.pad(jnp.swapaxes(ks_s.reshape(bs, ts, MLA_HEADS), 1, 2), ((0, 0), (0, 0), (0, PAGE - ts)))
        o_mla_s = _sample_attn("mla", l, page_table, qm_s[0], pad_new(km_s), (ksnew, wuv),
                               (cache_mla, ks_cache_t), npg)
        o_diff_s = _sample_attn("diff", l, page_table, qd_s[0], pad_new(kd_s),
                                (lam, bias_s, bias_new, subln, lam_init), (cache_diff,), npg)
        o_sb_s = _sample_attn("sb", l, page_table, qs_s[0], pad_new(kb_s), (umat_s, selsb),
                              (cache_sb,), npg)

        cat = lambda a, c: jnp.concatenate([a, c.astype(BF16)], axis=0)
        o_all = jnp.stack([cat(o_mla_p, o_mla_s), cat(o_pool_p, o_pool_s),
                           cat(o_diff_p, o_diff_s), cat(o_sb_p, o_sb_s)])

        merged = _merge(h, row(norm1_g[l]), w_gates, o_all, w_branch[l].astype(BF16))
        h = _outproj(merged, w_out[l].astype(BF16), h)
        h = _ffn(h, row(norm2_g[l]), ffn_w_gate[l].astype(BF16), ffn_w_up[l].astype(BF16),
                 ffn_w_down[l].astype(BF16))

        u_p = proj[:n_p, C_POOL:C_DQ].reshape(b, t, 512)
        outs["mla_p"].append(mrow_p.reshape(b, t, MLA_ROW))
        outs["ks_p"].append(ks_p.reshape(b, t, MLA_HEADS))
        outs["diff_p"].append(drow_p.reshape(b, t, DIFF_ROW))
        outs["sb_p"].append(proj[:n_p, C_SKV:C_END].reshape(b, t, SB_ROW))
        outs["pool_p"].append(jnp.concatenate([jnp.zeros((b, POOL_PAD, 512), F32), u_p], axis=1)[:, -POOL_PAD:])
        outs["mla_s"].append(mrow_s.reshape(bs, ts, MLA_ROW))
        outs["ks_s"].append(ks_s.reshape(bs, ts, MLA_HEADS))
        outs["diff_s"].append(drow_s.reshape(bs, ts, DIFF_ROW))
        outs["sb_s"].append(proj[n_p:, C_SKV:C_END].reshape(bs, ts, SB_ROW))
        outs["pool_s"].append(ext_s[:, -POOL_PAD:])

    st = lambda k: jnp.stack(outs[k])
    return (h[:n_p].reshape(b, t, d), h[n_p:].reshape(bs, ts, d),
            st("mla_p"), st("ks_p"), st("diff_p"), st("sb_p"), st("pool_p"),
            st("mla_s"), st("ks_s"), st("diff_s"), st("sb_s"), st("pool_s"))
```

```python
import functools
import math

import numpy as np
import jax
import jax.numpy as jnp
from jax import lax
from jax.experimental import pallas as pl
from jax.experimental.pallas import tpu as pltpu

F32 = jnp.float32
BF16 = jnp.bfloat16

N_BRANCH = 4
MLA_HEADS = 8
MLA_D_NOPE = 64
MLA_D_ROPE = 32
MLA_D_QK = MLA_D_NOPE + MLA_D_ROPE
MLA_D_V = 64
MLA_KV_RANK = 128
MLA_Q_RANK = 384
MLA_ROW = MLA_KV_RANK + MLA_D_ROPE
ROPE_BASE = 10000.0
POOL_WINDOWS = (2, 4, 8, 16)
POOL_GROUP = 128
POOL_PAD = max(POOL_WINDOWS) - 1
DIFF_HEADS = 4
DIFF_D = 64
DIFF_DV = 128
DIFF_ROW = 2 * DIFF_D + DIFF_DV
SB_HEADS = 8
SB_D = 64
SB_ROW = 2 * SB_D
REL_BUCKETS = 32
REL_MAX_DIST = 128
EPS = 1e-6
PAGE = 128

LANE = 128
MLA_QW = 256
VMEM_LIMIT = 56 * 1024 * 1024

C_CQ, C_CKV, C_POOL, C_DQ, C_SQ = 0, 384, 512, 1024, 1536
C_KR, C_DK, C_DV, C_SKV, C_END = 2048, 2176, 2304, 2432, 2560


def _dot(a, b):
    return jnp.dot(a, b, preferred_element_type=F32)


def _dot_nt(a, b):
    return lax.dot_general(a, b, (((1,), (1,)), ((), ())), preferred_element_type=F32)


def _dot_hilo(x, m):
    hi = x.astype(BF16)
    lo = (x - hi.astype(F32)).astype(BF16)
    return _dot(hi, m) + _dot(lo, m)


def _rms(x, g):
    return x * lax.rsqrt(jnp.mean(x * x, axis=-1, keepdims=True) + EPS) * g


def _pick(n, pref):
    t = min(n, pref)
    while n % t:
        t //= 2
    return t


def _cparams(sem):
    return pltpu.CompilerParams(dimension_semantics=sem, vmem_limit_bytes=VMEM_LIMIT)


def _inproj_kernel(x_ref, g_ref, w_ref, o_ref, xn_sc):
    @pl.when(pl.program_id(1) == 0)
    def _():
        xn_sc[...] = _rms(x_ref[...], g_ref[...]).astype(BF16)

    o_ref[...] = _dot(xn_sc[...], w_ref[...])


def _inproj(h, g, w):
    n, d = h.shape
    nout = w.shape[1]
    tm = _pick(n, 512)
    tn = _pick(nout, 512)
    return pl.pallas_call(
        _inproj_kernel,
        out_shape=jax.ShapeDtypeStruct((n, nout), F32),
        grid=(n // tm, nout // tn),
        in_specs=[pl.BlockSpec((tm, d), lambda i, j: (i, 0)),
                  pl.BlockSpec((1, d), lambda i, j: (0, 0)),
                  pl.BlockSpec((d, tn), lambda i, j: (0, j))],
        out_specs=pl.BlockSpec((tm, tn), lambda i, j: (i, j)),
        scratch_shapes=[pltpu.VMEM((tm, d), BF16)],
        compiler_params=_cparams(("arbitrary", "arbitrary")),
        name="inproj",
    )(h, g, w)


def _tok_math(p_ref, cos_ref, sin_ref, cqg, ckvg, wuq, qng, bd96, kng, knr, wuk, s8, dqg, bd64, dkg):
    tm = p_ref.shape[0]
    lane = lax.broadcasted_iota(jnp.int32, (tm, LANE), 1)
    cos = cos_ref[...]
    sin = sin_ref[...]

    cqn = _rms(p_ref[:, C_CQ:C_CKV], cqg[...]).astype(BF16)
    q = _dot(cqn, wuq[...])
    ss = _dot_hilo(q * q, bd96[...])
    qn = q * lax.rsqrt(ss * (1.0 / MLA_D_QK) + EPS) * qng[...]
    qnope = qn[:, :512] * kng[...]
    x1 = qn[:, 512:640]
    x2 = qn[:, 640:768]
    qall = jnp.concatenate([qnope, x1 * cos - x2 * sin, x1 * sin + x2 * cos], axis=1).astype(BF16)

    ckvn = _rms(p_ref[:, C_CKV:C_POOL], ckvg[...])
    knope = _dot(ckvn.astype(BF16), wuk[...])
    ssq = _dot_hilo(knope * knope, s8[...])
    kr = p_ref[:, C_KR:C_DK]
    krsq = jnp.sum(kr * kr, axis=-1, keepdims=True)
    kscale = lax.rsqrt((ssq + krsq) * (1.0 / MLA_D_QK) + EPS)
    krk = kr * knr[...]
    half = MLA_D_ROPE // 2
    rot = jnp.where(lane < half, -pltpu.roll(krk, LANE - half, 1), pltpu.roll(krk, half, 1))
    krrot = jnp.where(lane < MLA_D_ROPE, krk * cos + rot * sin, 0.0)

    dq = p_ref[:, C_DQ:C_SQ]
    ssd = _dot_hilo(dq * dq, bd64[...])
    dqn = dq * lax.rsqrt(ssd * (1.0 / DIFF_D) + EPS) * dqg[...]
    lo_half = lane < DIFF_D
    dk = p_ref[:, C_DK:C_DV]
    dk2 = dk * dk
    s_lo = jnp.sum(jnp.where(lo_half, dk2, 0.0), axis=-1, keepdims=True)
    s_hi = jnp.sum(jnp.where(lo_half, 0.0, dk2), axis=-1, keepdims=True)
    dkn = dk * lax.rsqrt(jnp.where(lo_half, s_lo, s_hi) * (1.0 / DIFF_D) + EPS) * dkg[...]
    return dict(qall=qall, ckvn=ckvn, kscale=kscale, krrot=krrot, dqn=dqn, dkn=dkn,
                dv=p_ref[:, C_DV:C_SKV], lane=lane, lo_half=lo_half)


def _tok_sample_kernel(p_ref, cos_ref, sin_ref, cqg, ckvg, wuq, qng, bd96, kng, knr, wuk, s8,
                       dqg, bd64, dkg, mq,
                       qmla_ref, qdiff_ref, qsb_ref, mrow_ref, ks_ref, drow_ref,
                       kmla_ref, kdiff_ref, ksb_ref):
    v = _tok_math(p_ref, cos_ref, sin_ref, cqg, ckvg, wuq, qng, bd96, kng, knr, wuk, s8, dqg, bd64, dkg)
    lo_half = v["lo_half"]
    qcat = _dot(v["qall"], mq[...])
    for h in range(MLA_HEADS):
        qmla_ref[0, h] = qcat[:, h * MLA_QW:(h + 1) * MLA_QW]
    ks_ref[...] = v["kscale"][:, :MLA_HEADS]
    mrow_ref[:, :MLA_KV_RANK] = v["ckvn"]
    mrow_ref[:, MLA_KV_RANK:] = v["krrot"][:, :MLA_D_ROPE]
    kmla_ref[:, :LANE] = v["ckvn"]
    kmla_ref[:, LANE:] = v["krrot"]
    for h in range(DIFF_HEADS):
        slab = v["dqn"][:, h * LANE:(h + 1) * LANE]
        qdiff_ref[0, 2 * h] = jnp.where(lo_half, slab, 0.0)
        qdiff_ref[0, 2 * h + 1] = jnp.where(lo_half, 0.0, slab)
    drow_ref[:, :LANE] = v["dkn"]
    drow_ref[:, LANE:] = v["dv"]
    kdiff_ref[:, :LANE] = v["dkn"]
    kdiff_ref[:, LANE:] = v["dv"]
    for hh in range(SB_HEADS // 2):
        slab = p_ref[:, C_SQ + hh * LANE:C_SQ + (hh + 1) * LANE]
        qsb_ref[0, 2 * hh] = jnp.where(lo_half, slab, 0.0)
        qsb_ref[0, 2 * hh + 1] = jnp.where(lo_half, pltpu.roll(slab, SB_D, 1), 0.0)
    ksb_ref[...] = p_ref[:, C_SKV:C_END]


def _tok_prompt_kernel(p_ref, cos_ref, sin_ref, cqg, ckvg, wuq, qng, bd96, kng, knr, wuk, s8,
                       dqg, bd64, dkg, mqt,
                       qmla_ref, qdiff_ref, qsb_ref, mrow_ref, ks_ref, drow_ref,
                       kmla_ref, ct_ref, ks3_ref, kdiff_ref, dvt_ref, ksb_ref, sbt_ref):
    tm = p_ref.shape[0]
    v = _tok_math(p_ref, cos_ref, sin_ref, cqg, ckvg, wuq, qng, bd96, kng, knr, wuk, s8, dqg, bd64, dkg)
    lane = v["lane"]
    row = lax.broadcasted_iota(jnp.int32, (LANE, tm), 0)
    top = row < DIFF_D
    qcat_t = _dot_nt(mqt[...], v["qall"])
    for h in range(MLA_HEADS):
        qmla_ref[0, h] = qcat_t[h * MLA_QW:(h + 1) * MLA_QW].astype(BF16)
    ks_ref[...] = v["kscale"][:, :MLA_HEADS]
    mrow_ref[:, :MLA_KV_RANK] = v["ckvn"]
    mrow_ref[:, MLA_KV_RANK:] = v["krrot"][:, :MLA_D_ROPE]
    kmla_ref[:, :LANE] = v["ckvn"].astype(BF16)
    kmla_ref[:, LANE:] = v["krrot"].astype(BF16)
    ct_ref[0, 0] = v["ckvn"].T.astype(BF16)
    ksm = jnp.where(lane < MLA_HEADS, v["kscale"] * (MLA_D_QK ** -0.5), 0.0)
    hi = ksm.astype(BF16)
    r1 = ksm - hi.astype(F32)
    mid = r1.astype(BF16)
    ks3_ref[:, :LANE] = hi
    ks3_ref[:, LANE:2 * LANE] = mid
    ks3_ref[:, 2 * LANE:] = (r1 - mid.astype(F32)).astype(BF16)
    for h in range(DIFF_HEADS):
        slab_t = (v["dqn"][:, h * LANE:(h + 1) * LANE] * (DIFF_D ** -0.5)).T
        qdiff_ref[0, 2 * h] = jnp.where(top, slab_t, 0.0).astype(BF16)
        qdiff_ref[0, 2 * h + 1] = jnp.where(top, 0.0, slab_t).astype(BF16)
    drow_ref[:, :LANE] = v["dkn"]
    drow_ref[:, LANE:] = v["dv"]
    kdiff_ref[...] = v["dkn"].astype(BF16)
    dvt_ref[0, 0] = v["dv"].T.astype(BF16)
    for hh in range(SB_HEADS // 2):
        slab_t = (p_ref[:, C_SQ + hh * LANE:C_SQ + (hh + 1) * LANE] * (SB_D ** -0.5)).T
        qsb_ref[0, 2 * hh] = jnp.where(top, slab_t, 0.0).astype(BF16)
        qsb_ref[0, 2 * hh + 1] = jnp.concatenate(
            [slab_t[SB_D:], jnp.zeros((LANE - SB_D, tm), F32)], axis=0).astype(BF16)
    skv = p_ref[:, C_SKV:C_END]
    ksb_ref[...] = skv.astype(BF16)
    sbt_ref[0, 0] = skv.T.astype(BF16)


def _tok_sample(proj, row0, n, cos, sin, consts):
    tm = _pick(n, 256)
    rb0 = row0 // tm
    assert row0 % tm == 0
    full = lambda a: pl.BlockSpec(a.shape, lambda i: (0,) * a.ndim)
    rows = lambda w: pl.BlockSpec((tm, w), lambda i: (i, 0))
    qspec = lambda w: pl.BlockSpec((1, 8, tm, w), lambda i: (0, 0, i, 0))
    out_shape = [
        jax.ShapeDtypeStruct((1, 8, n, MLA_QW), F32),
        jax.ShapeDtypeStruct((1, 8, n, LANE), F32),
        jax.ShapeDtypeStruct((1, 8, n, LANE), F32),
        jax.ShapeDtypeStruct((n, MLA_ROW), F32),
        jax.ShapeDtypeStruct((n, MLA_HEADS), F32),
        jax.ShapeDtypeStruct((n, DIFF_ROW), F32),
        jax.ShapeDtypeStruct((n, MLA_QW), F32),
        jax.ShapeDtypeStruct((n, DIFF_ROW), F32),
        jax.ShapeDtypeStruct((n, SB_ROW), F32),
    ]
    out_specs = [qspec(MLA_QW), qspec(LANE), qspec(LANE), rows(MLA_ROW), rows(MLA_HEADS),
                 rows(DIFF_ROW), rows(MLA_QW), rows(DIFF_ROW), rows(SB_ROW)]
    in_specs = [pl.BlockSpec((tm, C_END), lambda i: (rb0 + i, 0)), rows(LANE), rows(LANE)]
    in_specs += [full(c) for c in consts]
    return pl.pallas_call(
        _tok_sample_kernel, out_shape=out_shape, grid=(n // tm,), in_specs=in_specs, out_specs=out_specs,
        compiler_params=_cparams(("arbitrary",)), name="tok_sample",
    )(proj, cos, sin, *consts)


def _tok_prompt(proj, b, t, tm, cos, sin, consts):
    n = b * t
    nt = t // tm
    full = lambda a: pl.BlockSpec(a.shape, lambda bi, i: (0,) * a.ndim)
    rows = lambda w: pl.BlockSpec((tm, w), lambda bi, i: (bi * nt + i, 0))
    qspec = lambda w: pl.BlockSpec((1, 8, w, tm), lambda bi, i: (bi, 0, 0, i))
    tspec = pl.BlockSpec((1, 1, LANE, tm), lambda bi, i: (bi, i, 0, 0))
    tshape = jax.ShapeDtypeStruct((b, nt, LANE, tm), BF16)
    out_shape = [
        jax.ShapeDtypeStruct((b, 8, MLA_QW, t), BF16),
        jax.ShapeDtypeStruct((b, 8, LANE, t), BF16),
        jax.ShapeDtypeStruct((b, 8, LANE, t), BF16),
        jax.ShapeDtypeStruct((n, MLA_ROW), F32),
        jax.ShapeDtypeStruct((n, MLA_HEADS), F32),
        jax.ShapeDtypeStruct((n, DIFF_ROW), F32),
        jax.ShapeDtypeStruct((n, MLA_QW), BF16), tshape,
        jax.ShapeDtypeStruct((n, 3 * LANE), BF16),
        jax.ShapeDtypeStruct((n, LANE), BF16), tshape,
        jax.ShapeDtypeStruct((n, SB_ROW), BF16), tshape,
    ]
    out_specs = [qspec(MLA_QW), qspec(LANE), qspec(LANE), rows(MLA_ROW), rows(MLA_HEADS), rows(DIFF_ROW),
                 rows(MLA_QW), tspec, rows(3 * LANE), rows(LANE), tspec, rows(SB_ROW), tspec]
    in_specs = [pl.BlockSpec((tm, C_END), lambda bi, i: (bi * nt + i, 0)),
                pl.BlockSpec((tm, LANE), lambda bi, i: (i, 0)),
                pl.BlockSpec((tm, LANE), lambda bi, i: (i, 0))] + [full(c) for c in consts]
    return pl.pallas_call(
        _tok_prompt_kernel, out_shape=out_shape, grid=(b, nt), in_specs=in_specs, out_specs=out_specs,
        compiler_params=_cparams(("arbitrary", "arbitrary")), name="tok_prompt",
    )(proj, cos, sin, *consts)


def _pool_prompt_kernel(u_ref, halo_ref, w_ref, sc_ref, o_ref, ext_sc):
    tm = u_ref.shape[0]
    ti = pl.program_id(1)
    u = u_ref[...]
    ext_sc[16:, :] = u
    ext_sc[:16, :] = jnp.where(ti == 0, 0.0, halo_ref[...])
    pos = ti * tm + lax.broadcasted_iota(jnp.int32, (tm, 1), 0)
    outs = []
    for g, w in enumerate(POOL_WINDOWS):
        c0 = g * POOL_GROUP
        wsum = u[:, c0:c0 + POOL_GROUP]
        for d in range(1, w):
            wsum = wsum + ext_sc[16 - d:16 - d + tm, c0:c0 + POOL_GROUP]
        cnt = jnp.minimum(pos + 1, w).astype(F32)
        pooled = wsum / cnt - u[:, c0:c0 + POOL_GROUP]
        outs.append(_dot(pooled.astype(BF16), w_ref[g]))
    o_ref[...] = (jnp.concatenate(outs, axis=1) * sc_ref[...]).astype(o_ref.dtype)


def _pool_prompt(proj, b, t, w, scale):
    tm = _pick(t, 512)
    nt = t // tm
    cb = C_POOL // 512
    return pl.pallas_call(
        _pool_prompt_kernel,
        out_shape=jax.ShapeDtypeStruct((b * t, 512), BF16),
        grid=(b, nt),
        in_specs=[pl.BlockSpec((tm, 512), lambda bi, i: (bi * nt + i, cb)),
                  pl.BlockSpec((16, 512), lambda bi, i: (jnp.maximum((bi * nt + i) * (tm // 16) - 1, 0), cb)),
                  pl.BlockSpec(w.shape, lambda bi, i: (0, 0, 0)),
                  pl.BlockSpec(scale.shape, lambda bi, i: (0, 0))],
        out_specs=pl.BlockSpec((tm, 512), lambda bi, i: (bi * nt + i, 0)),
        scratch_shapes=[pltpu.VMEM((tm + 16, 512), F32)],
        compiler_params=_cparams(("arbitrary", "arbitrary")),
        name="pool_prompt",
    )(proj, proj, w, scale)


def _pool_sample_kernel(ext_ref, w_ref, sc_ref, o_ref, *, n_past):
    ts = o_ref.shape[0]
    for t in range(ts):
        outs = []
        for g, w in enumerate(POOL_WINDOWS):
            c0 = g * POOL_GROUP
            cur = ext_ref[POOL_PAD + t, :, c0:c0 + POOL_GROUP]
            wsum = cur
            for d in range(1, w):
                wsum = wsum + ext_ref[POOL_PAD + t - d, :, c0:c0 + POOL_GROUP]
            cnt = float(min(n_past + t + 1, w))
            pooled = wsum / cnt - cur
            outs.append(_dot(pooled.astype(BF16), w_ref[g]))
        o_ref[t] = jnp.concatenate(outs, axis=1) * sc_ref[...]


def _pool_sample(ext, w, scale, n_past):
    rows, seqs, _ = ext.shape
    ts = rows - POOL_PAD
    return pl.pallas_call(
        functools.partial(_pool_sample_kernel, n_past=n_past),
        out_shape=jax.ShapeDtypeStruct((ts, seqs, 512), F32),
        grid=(1,),
        in_specs=[pl.BlockSpec(ext.shape, lambda i: (0, 0, 0)),
                  pl.BlockSpec(w.shape, lambda i: (0, 0, 0)),
                  pl.BlockSpec(scale.shape, lambda i: (0, 0))],
        out_specs=pl.BlockSpec((ts, seqs, 512), lambda i: (0, 0, 0)),
        compiler_params=_cparams(("arbitrary",)),
        name="pool_sample",
    )(ext, w, scale)


def _col_qpos(qi, tq, ncol):
    return qi * tq + (lax.broadcasted_iota(jnp.int32, (1, ncol), 1) & (tq - 1))


def _softmax_step_t(s, vt, m_sc, l_sc, acc_sc):
    m_old = m_sc[...]
    m_new = jnp.maximum(m_old, jnp.max(s, axis=0, keepdims=True))
    alpha = jnp.exp(m_old - m_new)
    p = jnp.exp(s - m_new)
    l_sc[...] = alpha * l_sc[...] + jnp.sum(p, axis=0, keepdims=True)
    acc_sc[...] = alpha * acc_sc[...] + _dot(vt, p.astype(BF16))
    m_sc[...] = m_new


def _mla_prompt_kernel(qt_ref, k_ref, ct_ref, ks3_ref, e3_ref, wuvt_ref, o_ref, m_sc, l_sc, acc_sc, *, tq, tk):
    qi = pl.program_id(1)
    ncol = MLA_HEADS * tq
    qt = jnp.concatenate([qt_ref[0, h] for h in range(MLA_HEADS)], axis=1)
    m_sc[...] = jnp.full(m_sc.shape, -jnp.inf, F32)
    l_sc[...] = jnp.zeros(l_sc.shape, F32)
    acc_sc[...] = jnp.zeros(acc_sc.shape, F32)
    nkb = ((qi + 1) * tq + tk - 1) // tk
    qpos = _col_qpos(qi, tq, ncol)

    def step(kb, masked):
        start = pl.multiple_of(kb * tk, tk)
        s = _dot(k_ref[0, pl.ds(start, tk), :], qt)
        s = s * _dot(ks3_ref[0, pl.ds(start, tk), :], e3_ref[...])
        if masked:
            kpos = kb * tk + lax.broadcasted_iota(jnp.int32, (tk, 1), 0)
            s = jnp.where(kpos <= qpos, s, -jnp.inf)
        _softmax_step_t(s, ct_ref[0, kb], m_sc, l_sc, acc_sc)

    def body(kb, carry):
        step(kb, False)
        return carry

    lax.fori_loop(0, nkb - 1, body, 0)
    step(nkb - 1, True)
    o = acc_sc[...] / l_sc[...]
    ocat = jnp.concatenate([o[:, h * tq:(h + 1) * tq] for h in range(MLA_HEADS)], axis=0)
    o_ref[...] = _dot(wuvt_ref[...], ocat.astype(BF16)).T.astype(o_ref.dtype)


def _diff_prompt_kernel(lam_ref, qt_ref, k_ref, vt_ref, bias_ref, g_ref, o_ref, m_sc, l_sc, acc_sc,
                        *, tq, tk, lam_init):
    qi = pl.program_id(1)
    nh = 2 * DIFF_HEADS
    ncol = nh * tq
    qt = jnp.concatenate([qt_ref[0, j] for j in range(nh)], axis=1)
    m_sc[...] = jnp.full(m_sc.shape, -jnp.inf, F32)
    l_sc[...] = jnp.zeros(l_sc.shape, F32)
    acc_sc[...] = jnp.zeros(acc_sc.shape, F32)
    nkb = ((qi + 1) * tq + tk - 1) // tk
    ntile = bias_ref.shape[0]
    qpos = _col_qpos(qi, tq, ncol)

    def step(kb, masked):
        start = pl.multiple_of(kb * tk, tk)
        s = _dot(k_ref[0, pl.ds(start, tk), :], qt)
        bt = bias_ref[jnp.minimum((qi * tq - kb * tk) // tq, ntile - 1)]
        s = s + jnp.concatenate([bt[h] for h in range(DIFF_HEADS) for _ in range(2)], axis=1)
        if masked:
            kpos = kb * tk + lax.broadcasted_iota(jnp.int32, (tk, 1), 0)
            s = jnp.where(kpos <= qpos, s, -jnp.inf)
        _softmax_step_t(s, vt_ref[0, kb], m_sc, l_sc, acc_sc)

    def body(kb, carry):
        step(kb, False)
        return carry

    lax.fori_loop(0, nkb - 1, body, 0)
    step(nkb - 1, True)
    o = acc_sc[...] / l_sc[...]
    lam = lam_ref[0]
    outs = []
    for h in range(DIFF_HEADS):
        oh = o[:, (2 * h) * tq:(2 * h + 1) * tq] - lam * o[:, (2 * h + 1) * tq:(2 * h + 2) * tq]
        oh = oh * lax.rsqrt(jnp.mean(oh * oh, axis=0, keepdims=True) + EPS) * g_ref[...]
        outs.append(oh * (1.0 - lam_init))
    o_ref[...] = jnp.concatenate(outs, axis=0).T.astype(o_ref.dtype)


def _softplus(z):
    return jnp.maximum(z, 0.0) + jnp.log1p(jnp.exp(-jnp.abs(z)))


def _sb_prompt_kernel(qt_ref, k_ref, vt_ref, lmat_ref, o_ref, r_sc, acc_sc, *, tq, tk):
    qi = pl.program_id(1)
    ncol = SB_HEADS * tq
    qt = jnp.concatenate([qt_ref[0, h] for h in range(SB_HEADS)], axis=1)
    r_sc[...] = jnp.zeros(r_sc.shape, F32)
    acc_sc[...] = jnp.zeros(acc_sc.shape, F32)
    nkb = ((qi + 1) * tq + tk - 1) // tk
    qpos = _col_qpos(qi, tq, ncol)

    def step(kb, masked):
        start = pl.multiple_of(kb * tk, tk)
        z = _dot(k_ref[0, pl.ds(start, tk), :], qt)
        sp = _softplus(z)
        lk = -sp
        if masked:
            vis = kb * tk + lax.broadcasted_iota(jnp.int32, (tk, 1), 0) < qpos
            lk = jnp.where(vis, lk, 0.0)
        hi = lk.astype(BF16)
        lo = (lk - hi.astype(F32)).astype(BF16)
        suf = _dot(lmat_ref[...], hi) + _dot(lmat_ref[...], lo)
        a = jnp.exp((z - sp) + suf + r_sc[...])
        if masked:
            a = jnp.where(vis, a, 0.0)
        acc_sc[...] += _dot(vt_ref[0, kb, SB_D:, :], a.astype(BF16))
        r_sc[...] += jnp.sum(lk, axis=0, keepdims=True)

    step(nkb - 1, True)

    def body(it, carry):
        step(nkb - 2 - it, False)
        return carry

    lax.fori_loop(0, nkb - 1, body, 0)
    acc = acc_sc[...]
    ocat = jnp.concatenate([acc[:, h * tq:(h + 1) * tq] for h in range(SB_HEADS)], axis=0)
    o_ref[...] = ocat.T.astype(o_ref.dtype)


def _prompt_attn(kind, qt, krows, vt, extra, b, t, tq, tk, **kw):
    nq = t // tq
    w = qt.shape[2]
    kw_ = krows.shape[-1]
    k3 = krows.reshape(b, t, kw_)
    qspec = pl.BlockSpec((1, 8, w, tq), lambda bi, i, *_: (bi, 0, 0, i))
    kspec = pl.BlockSpec((1, t, kw_), lambda bi, i, *_: (bi, 0, 0))
    vspec = pl.BlockSpec((1,) + vt.shape[1:], lambda bi, i, *_: (bi, 0, 0, 0))
    ospec = pl.BlockSpec((tq, 512), lambda bi, i, *_: (bi * nq + i, 0))
    full = lambda a: pl.BlockSpec(a.shape, lambda bi, i, *_: (0,) * a.ndim)
    ncol = 8 * tq
    out_shape = jax.ShapeDtypeStruct((b * t, 512), BF16)
    stats = [pltpu.VMEM((1, ncol), F32), pltpu.VMEM((1, ncol), F32), pltpu.VMEM((LANE, ncol), F32)]
    if kind == "mla":
        ks3, e3, wuvt = extra
        ks3 = ks3.reshape(b, t, 3 * LANE)
        return pl.pallas_call(
            functools.partial(_mla_prompt_kernel, tq=tq, tk=tk),
            out_shape=out_shape, grid=(b, nq),
            in_specs=[qspec, kspec, vspec, pl.BlockSpec((1, t, 3 * LANE), lambda bi, i: (bi, 0, 0)),
                      full(e3), full(wuvt)],
            out_specs=ospec, scratch_shapes=stats,
            compiler_params=_cparams(("arbitrary", "arbitrary")), name="mla_prompt",
        )(qt, k3, vt, ks3, e3, wuvt)
    if kind == "diff":
        lam, bias, g = extra
        return pl.pallas_call(
            functools.partial(_diff_prompt_kernel, tq=tq, tk=tk, lam_init=kw["lam_init"]),
            out_shape=out_shape,
            grid_spec=pltpu.PrefetchScalarGridSpec(
                num_scalar_prefetch=1, grid=(b, nq),
                in_specs=[qspec, kspec, vspec, full(bias), full(g)],
                out_specs=ospec, scratch_shapes=stats),
            compiler_params=_cparams(("arbitrary", "arbitrary")), name="diff_prompt",
        )(lam, qt, k3, vt, bias, g)
    (lmat,) = extra
    return pl.pallas_call(
        functools.partial(_sb_prompt_kernel, tq=tq, tk=tk),
        out_shape=out_shape, grid=(b, nq),
        in_specs=[qspec, kspec, vspec, full(lmat)],
        out_specs=ospec,
        scratch_shapes=[pltpu.VMEM((1, ncol), F32), pltpu.VMEM((SB_D, ncol), F32)],
        compiler_params=_cparams(("arbitrary", "arbitrary")), name="sb_prompt",
    )(qt, k3, vt, lmat)


def _page_copies(cache, layer, pt_ref, seq, chunk, buf, sem, slot, npg):
    return [pltpu.make_async_copy(cache.at[layer, pt_ref[seq, chunk * npg + i]],
                                  buf.at[slot, i], sem.at[slot]) for i in range(npg)]


def _paged_loop(caches, bufs, sems, layer, pt_ref, nseq, nch, npg, compute, reverse):
    seq = pl.program_id(0)

    def chunk_of(c):
        return nch - 1 - c if reverse else c

    def start(sq, c, slot):
        for cache, buf, sem in zip(caches, bufs, sems):
            for cp in _page_copies(cache, layer, pt_ref, sq, chunk_of(c), buf, sem, slot, npg):
                cp.start()

    def wait(sq, c, slot):
        for cache, buf, sem in zip(caches, bufs, sems):
            for cp in _page_copies(cache, layer, pt_ref, sq, chunk_of(c), buf, sem, slot, npg):
                cp.wait()

    @pl.when(seq == 0)
    def _():
        start(seq, 0, 0)

    def body(c, carry):
        slot = (seq * nch + c) % 2
        last = c == nch - 1
        nseq_ = jnp.where(last, seq + 1, seq)
        nc = jnp.where(last, 0, c + 1)

        @pl.when(nseq_ < nseq)
        def _():
            start(nseq_, nc, 1 - slot)

        wait(seq, c, slot)
        return compute(chunk_of(c), slot, carry)

    return body


def _sb_block(z, mask, v, umat, r):
    sp = _softplus(z)
    lk = -sp
    if mask is not None:
        shape3 = (-1,) + mask.shape[1:]
        lk = jnp.where(mask, lk.reshape(shape3), 0.0).reshape(z.shape)
    suf = _dot_hilo(lk, umat)
    a = jnp.exp((z - sp) + suf + r)
    if mask is not None:
        a = jnp.where(mask, a.reshape(shape3), 0.0).reshape(z.shape)
    return _dot(a.astype(BF16), v), r + jnp.sum(lk, axis=-1, keepdims=True)


def _mla_sample_kernel(pt_ref, q_ref, knew_ref, ksnew_ref, wuv_ref, rows_hbm, ks_hbm, o_ref,
                       rbuf, kbuf, rsem, ksem, *, layer, nseq, nch, npg):
    ts = q_ref.shape[1]
    nr = MLA_HEADS * ts
    tk = npg * PAGE
    q = q_ref[...].reshape(nr, MLA_QW).astype(BF16)
    qc = q[:, :MLA_KV_RANK]
    qr = q[:, MLA_KV_RANK:]
    scale = MLA_D_QK ** -0.5

    def update(s, v, carry):
        m, l, acc = carry
        m_new = jnp.maximum(m, jnp.max(s, axis=-1, keepdims=True))
        alpha = jnp.exp(m - m_new)
        p = jnp.exp(s - m_new)
        return (m_new, alpha * l + jnp.sum(p, axis=-1, keepdims=True),
                alpha * acc + _dot(p.astype(BF16), v))

    def compute(chunk, slot, carry):
        rows = rbuf[slot].reshape(tk, MLA_ROW).astype(BF16)
        c = rows[:, :MLA_KV_RANK]
        krp = jnp.concatenate([rows[:, MLA_KV_RANK:], jnp.zeros((tk, LANE - MLA_D_ROPE), BF16)], axis=1)
        s = (_dot_nt(qc, c) + _dot_nt(qr, krp)).reshape(MLA_HEADS, ts, tk)
        ks = jnp.concatenate([kbuf[slot, i].T for i in range(npg)], axis=1) * scale
        s = (s * ks[:, None, :]).reshape(nr, tk)
        return update(s, c, carry)

    body = _paged_loop((rows_hbm, ks_hbm), (rbuf, kbuf), (rsem, ksem), layer, pt_ref, nseq, nch, npg,
                       compute, reverse=False)
    init = (jnp.full((nr, 1), -jnp.inf, F32), jnp.zeros((nr, 1), F32), jnp.zeros((nr, MLA_KV_RANK), F32))
    carry = lax.fori_loop(0, nch, body, init)

    kn = knew_ref[0]
    s = _dot_nt(q, kn).reshape(MLA_HEADS, ts, PAGE)
    s = s * (ksnew_ref[0] * scale)[:, None, :]
    t_i = lax.broadcasted_iota(jnp.int32, (1, ts, PAGE), 1)
    j_i = lax.broadcasted_iota(jnp.int32, (1, ts, PAGE), 2)
    s = jnp.where(j_i <= t_i, s, -jnp.inf).reshape(nr, PAGE)
    m, l, acc = update(s, kn[:, :MLA_KV_RANK], carry)
    o = acc / l
    ocat = jnp.concatenate([o[h * ts:(h + 1) * ts] for h in range(MLA_HEADS)], axis=1)
    o_ref[...] = _dot(ocat.astype(BF16), wuv_ref[...])


def _diff_sample_kernel(pt_ref, lam_ref, q_ref, knew_ref, bias_ref, bnew_ref, g_ref, rows_hbm, o_ref,
                        rbuf, rsem, *, layer, nseq, nch, npg, lam_init):
    ts = q_ref.shape[1]
    nh = 2 * DIFF_HEADS
    nr = nh * ts
    tk = npg * PAGE
    q = q_ref[...].reshape(nr, LANE).astype(BF16)

    def update(s, v, carry):
        m, l, acc = carry
        m_new = jnp.maximum(m, jnp.max(s, axis=-1, keepdims=True))
        alpha = jnp.exp(m - m_new)
        p = jnp.exp(s - m_new)
        return (m_new, alpha * l + jnp.sum(p, axis=-1, keepdims=True),
                alpha * acc + _dot(p.astype(BF16), v))

    def compute(chunk, slot, carry):
        rows = rbuf[slot].reshape(tk, DIFF_ROW).astype(BF16)
        s = _dot_nt(q, rows[:, :LANE]).reshape(DIFF_HEADS, 2, ts, tk) * (DIFF_D ** -0.5)
        case = jnp.where(chunk == nch - 1, 1, 0)
        s = (s + bias_ref[case][:, None, :, :]).reshape(nr, tk)
        return update(s, rows[:, LANE:], carry)

    body = _paged_loop((rows_hbm,), (rbuf,), (rsem,), layer, pt_ref, nseq, nch, npg, compute, reverse=False)
    init = (jnp.full((nr, 1), -jnp.inf, F32), jnp.zeros((nr, 1), F32), jnp.zeros((nr, DIFF_DV), F32))
    carry = lax.fori_loop(0, nch, body, init)

    kn = knew_ref[0]
    s = _dot_nt(q, kn[:, :LANE]).reshape(DIFF_HEADS, 2, ts, PAGE) * (DIFF_D ** -0.5)
    s = s + bnew_ref[...][:, None, :, :]
    t_i = lax.broadcasted_iota(jnp.int32, (1, 1, ts, PAGE), 2)
    j_i = lax.broadcasted_iota(jnp.int32, (1, 1, ts, PAGE), 3)
    s = jnp.where(j_i <= t_i, s, -jnp.inf).reshape(nr, PAGE)
    m, l, acc = update(s, kn[:, LANE:], carry)
    o = acc / l
    lam = lam_ref[0]
    outs = []
    for h in range(DIFF_HEADS):
        oh = o[(2 * h) * ts:(2 * h + 1) * ts] - lam * o[(2 * h + 1) * ts:(2 * h + 2) * ts]
        outs.append(_rms(oh, g_ref[...]) * (1.0 - lam_init))
    o_ref[...] = jnp.concatenate(outs, axis=1)


def _sb_sample_kernel(pt_ref, q_ref, knew_ref, u_ref, uo_ref, sel_ref, rows_hbm, o_ref, rbuf, rsem,
                      *, layer, nseq, nch, npg):
    ts = q_ref.shape[1]
    nr = SB_HEADS * ts
    tk = npg * PAGE
    q = q_ref[...].reshape(nr, LANE).astype(BF16)

    kn = knew_ref[0]
    z = _dot_nt(q, kn) * (SB_D ** -0.5)
    t_i = lax.broadcasted_iota(jnp.int32, (1, ts, PAGE), 1)
    j_i = lax.broadcasted_iota(jnp.int32, (1, ts, PAGE), 2)
    acc0, r0 = _sb_block(z, j_i < t_i, kn, u_ref[...], jnp.zeros((nr, 1), F32))

    def compute(chunk, slot, carry):
        acc, roff = carry
        rows = rbuf[slot].reshape(tk, SB_ROW).astype(BF16)
        z = _dot_nt(q, rows) * (SB_D ** -0.5)
        sp = _softplus(z)
        lk = -sp
        hi = lk.astype(BF16)
        lo = (lk - hi.astype(F32)).astype(BF16)
        sufs = [None] * npg
        for i in reversed(range(npg)):
            sl = slice(i * PAGE, (i + 1) * PAGE)
            both = _dot(hi[:, sl], uo_ref[...]) + _dot(lo[:, sl], uo_ref[...])
            sufs[i] = both[:, :PAGE] + roff
            roff = roff + both[:, PAGE:]
        a = jnp.exp((z - sp) + jnp.concatenate(sufs, axis=1))
        return acc + _dot(a.astype(BF16), rows), roff

    body = _paged_loop((rows_hbm,), (rbuf,), (rsem,), layer, pt_ref, nseq, nch, npg, compute, reverse=True)
    acc, _ = lax.fori_loop(0, nch, body, (acc0, jnp.broadcast_to(r0, (nr, PAGE))))
    ocat = jnp.concatenate([acc[h * ts:(h + 1) * ts] for h in range(SB_HEADS)], axis=1)
    o_ref[...] = _dot(ocat.astype(BF16), sel_ref[...])


def _sample_attn(kind, layer, pt, q, knew, extra, caches, npg):
    nseq, npages = pt.shape
    nch = npages // npg
    ts = q.shape[1] // nseq
    w = q.shape[-1]
    kw_ = knew.shape[-1]
    qspec = pl.BlockSpec((8, ts, w), lambda s, *_: (0, s, 0))
    knspec = pl.BlockSpec((1, PAGE, kw_), lambda s, *_: (s, 0, 0))
    ospec = pl.BlockSpec((ts, 512), lambda s, *_: (s, 0))
    full = lambda a: pl.BlockSpec(a.shape, lambda s, *_: (0,) * a.ndim)
    anyspec = pl.BlockSpec(memory_space=pl.ANY)
    out_shape = jax.ShapeDtypeStruct((nseq * ts, 512), F32)
    common = dict(layer=layer, nseq=nseq, nch=nch, npg=npg)
    if kind == "mla":
        ksnew, wuv = extra
        rows_c, ks_c = caches
        gs = pltpu.PrefetchScalarGridSpec(
            num_scalar_prefetch=1, grid=(nseq,),
            in_specs=[qspec, knspec, pl.BlockSpec((1, 8, PAGE), lambda s, *_: (s, 0, 0)), full(wuv),
                      anyspec, anyspec],
            out_specs=ospec,
            scratch_shapes=[pltpu.VMEM((2, npg, PAGE, MLA_ROW), F32), pltpu.VMEM((2, npg, PAGE, MLA_HEADS), F32),
                            pltpu.SemaphoreType.DMA((2,)), pltpu.SemaphoreType.DMA((2,))])
        return pl.pallas_call(functools.partial(_mla_sample_kernel, **common), out_shape=out_shape,
                              grid_spec=gs, compiler_params=_cparams(("arbitrary",)),
                              name="mla_sample")(pt, q, knew, ksnew, wuv, rows_c, ks_c)
    if kind == "diff":
        lam, bias, bnew, g, lam_init = extra
        (rows_c,) = caches
        gs = pltpu.PrefetchScalarGridSpec(
            num_scalar_prefetch=2, grid=(nseq,),
            in_specs=[qspec, knspec, full(bias), full(bnew), full(g), anyspec],
            out_specs=ospec,
            scratch_shapes=[pltpu.VMEM((2, npg, PAGE, DIFF_ROW), F32), pltpu.SemaphoreType.DMA((2,))])
        return pl.pallas_call(functools.partial(_diff_sample_kernel, lam_init=lam_init, **common),
                              out_shape=out_shape, grid_spec=gs, compiler_params=_cparams(("arbitrary",)),
                              name="diff_sample")(pt, lam, q, knew, bias, bnew, g, rows_c)
    umat, uo, sel = extra
    (rows_c,) = caches
    gs = pltpu.PrefetchScalarGridSpec(
        num_scalar_prefetch=1, grid=(nseq,),
        in_specs=[qspec, knspec, full(umat), full(uo), full(sel), anyspec],
        out_specs=ospec,
        scratch_shapes=[pltpu.VMEM((2, npg, PAGE, SB_ROW), F32), pltpu.SemaphoreType.DMA((2,))])
    return pl.pallas_call(functools.partial(_sb_sample_kernel, **common), out_shape=out_shape,
                          grid_spec=gs, compiler_params=_cparams(("arbitrary",)),
                          name="sb_sample")(pt, q, knew, umat, uo, sel, rows_c)


def _merge_kernel(h_ref, g_ref, wg0, wg1, wg2, wg3, o_ref, wb_ref, out_ref, xn_sc):
    @pl.when(pl.program_id(1) == 0)
    def _():
        xn_sc[...] = _rms(h_ref[...], g_ref[...]).astype(BF16)

    xn = xn_sc[...]
    acc = None
    for g, wg in enumerate((wg0, wg1, wg2, wg3)):
        gate = jax.nn.sigmoid(_dot(xn, wg[...]))
        term = gate * _dot(o_ref[g], wb_ref[g])
        acc = term if acc is None else acc + term
    out_ref[...] = acc.astype(out_ref.dtype)


def _merge(h, g, wgates, o_all, wb):
    n, d = h.shape
    tm = _pick(n, 512)
    tn = _pick(d, 512)
    nj = d // tn
    bw = o_all.shape[-1]
    gspec = lambda gi: pl.BlockSpec((d, tn), lambda i, j: (0, gi * nj + j))
    return pl.pallas_call(
        _merge_kernel,
        out_shape=jax.ShapeDtypeStruct((n, d), BF16),
        grid=(n // tm, nj),
        in_specs=[pl.BlockSpec((tm, d), lambda i, j: (i, 0)),
                  pl.BlockSpec((1, d), lambda i, j: (0, 0)),
                  gspec(0), gspec(1), gspec(2), gspec(3),
                  pl.BlockSpec((N_BRANCH, tm, bw), lambda i, j: (0, i, 0)),
                  pl.BlockSpec((N_BRANCH, bw, tn), lambda i, j: (0, 0, j))],
        out_specs=pl.BlockSpec((tm, tn), lambda i, j: (i, j)),
        scratch_shapes=[pltpu.VMEM((tm, d), BF16)],
        compiler_params=_cparams(("arbitrary", "arbitrary")),
        name="merge",
    )(h, g, wgates, wgates, wgates, wgates, o_all, wb)


def _outproj_kernel(m_ref, w_ref, h_ref, o_ref):
    o_ref[...] = h_ref[...] + _dot(m_ref[...], w_ref[...])


def _outproj(merged, w, h):
    n, d = h.shape
    tm = _pick(n, 512)
    tn = _pick(d, 1024)
    return pl.pallas_call(
        _outproj_kernel,
        out_shape=jax.ShapeDtypeStruct((n, d), F32),
        grid=(n // tm, d // tn),
        in_specs=[pl.BlockSpec((tm, d), lambda i, j: (i, 0)),
                  pl.BlockSpec((d, tn), lambda i, j: (0, j)),
                  pl.BlockSpec((tm, tn), lambda i, j: (i, j))],
        out_specs=pl.BlockSpec((tm, tn), lambda i, j: (i, j)),
        compiler_params=_cparams(("arbitrary", "arbitrary")),
        name="outproj",
    )(merged, w, h)


def _ffn_kernel(h_ref, g_ref, wg_ref, wu_ref, wd_ref, o_ref, hn_sc, acc_sc):
    f = pl.program_id(1)

    @pl.when(f == 0)
    def _():
        h = h_ref[...]
        hn_sc[...] = _rms(h, g_ref[...]).astype(BF16)
        acc_sc[...] = h

    hn = hn_sc[...]
    a = jax.nn.silu(_dot(hn, wg_ref[...])) * _dot(hn, wu_ref[...])
    acc_sc[...] += _dot(a.astype(BF16), wd_ref[...])

    @pl.when(f == pl.num_programs(1) - 1)
    def _():
        o_ref[...] = acc_sc[...]


def _ffn(h, g, wg, wu, wd):
    n, d = h.shape
    dff = wg.shape[1]
    tm = _pick(n, 512)
    tf = 512 if dff % 512 == 0 else _pick(dff, 256)
    return pl.pallas_call(
        _ffn_kernel,
        out_shape=jax.ShapeDtypeStruct((n, d), F32),
        grid=(n // tm, dff // tf),
        in_specs=[pl.BlockSpec((tm, d), lambda i, f: (i, 0)),
                  pl.BlockSpec((1, d), lambda i, f: (0, 0)),
                  pl.BlockSpec((d, tf), lambda i, f: (0, f)),
                  pl.BlockSpec((d, tf), lambda i, f: (0, f)),
                  pl.BlockSpec((tf, d), lambda i, f: (f, 0))],
        out_specs=pl.BlockSpec((tm, d), lambda i, f: (i, 0)),
        scratch_shapes=[pltpu.VMEM((tm, d), BF16), pltpu.VMEM((tm, d), F32)],
        compiler_params=_cparams(("arbitrary", "arbitrary")),
        name="ffn",
    )(h, g, wg, wu, wd)


def _rope_tables(pos):
    half = MLA_D_ROPE // 2
    inv = ROPE_BASE ** (-jnp.arange(half, dtype=F32) / half)
    ang = pos.astype(F32)[:, None] * inv[None, :]
    reps = LANE // half
    return jnp.tile(jnp.cos(ang), (1, reps)), jnp.tile(jnp.sin(ang), (1, reps))


def _bias_by_distance(rel_bias, n):
    exact = REL_BUCKETS // 2
    n = jnp.maximum(n, 0)
    nf = jnp.maximum(n, 1).astype(F32)
    large = exact + (jnp.log(nf / exact) / math.log(REL_MAX_DIST / exact)
                     * (REL_BUCKETS - exact)).astype(jnp.int32)
    bucket = jnp.where(n < exact, n, jnp.minimum(large, REL_BUCKETS - 1))
    return jnp.moveaxis(rel_bias[bucket].astype(F32), -1, 0)


def _static_mats():
    col = np.arange(768)
    head = np.where(col < 512, col // 64, np.where(col < 640, (col - 512) // 16, (col - 640) // 16))
    bd96 = (head[:, None] == head[None, :]).astype(np.float32)
    c5 = np.arange(512)
    bd64 = (c5[:, None] // 64 == c5[None, :] // 64).astype(np.float32)
    s8 = np.zeros((512, LANE), np.float32)
    s8[c5, c5 // 64] = 1.0
    half = MLA_D_ROPE // 2
    selq = np.zeros((256, MLA_HEADS * MLA_QW), np.float32)
    for h in range(MLA_HEADS):
        for i in range(half):
            selq[h * half + i, h * MLA_QW + MLA_KV_RANK + i] = 1.0
            selq[128 + h * half + i, h * MLA_QW + MLA_KV_RANK + half + i] = 1.0
    selsb = np.zeros((SB_HEADS * LANE, 512), np.float32)
    for h in range(SB_HEADS):
        for d in range(SB_D):
            selsb[h * LANE + SB_D + d, h * SB_D + d] = 1.0
    return bd96, bd64, s8, selq, selsb


def _suffix_matrix(tk):
    j = np.arange(tk)
    return (j[:, None] > j[None, :]).astype(np.float32)


def _layer_consts(l, P):
    bd96, bd64, s8, selq, selsb = _static_mats()
    row = lambda v: v.reshape(1, -1).astype(F32)
    w_uq = P["mla_w_uq"][l]
    wuq = jnp.concatenate([w_uq[:, :, :MLA_D_NOPE].reshape(MLA_Q_RANK, 512),
                           w_uq[:, :, MLA_D_NOPE:MLA_D_NOPE + 16].reshape(MLA_Q_RANK, 128),
                           w_uq[:, :, MLA_D_NOPE + 16:].reshape(MLA_Q_RANK, 128)], axis=1).astype(BF16)
    qn = P["mla_qn_g"][l]
    qng = row(jnp.concatenate([jnp.tile(qn[:64], 8), jnp.tile(qn[64:80], 8), jnp.tile(qn[80:], 8)]))
    kn = P["mla_kn_g"][l]
    kng = row(jnp.tile(kn[:MLA_D_NOPE], 8))
    knr = row(jnp.concatenate([kn[MLA_D_NOPE:], jnp.zeros((LANE - MLA_D_ROPE,), F32)]))
    w_uk = P["mla_w_uk"][l]
    blk = jnp.pad(jnp.transpose(w_uk, (1, 2, 0)), ((0, 0), (0, 0), (0, MLA_QW - MLA_KV_RANK)))
    m_nope = jnp.einsum("hjc,hg->hjgc", blk, jnp.eye(MLA_HEADS, dtype=F32)).reshape(512, MLA_HEADS * MLA_QW)
    mq = jnp.concatenate([m_nope, jnp.asarray(selq)], axis=0).astype(BF16)
    wuk = w_uk.reshape(MLA_KV_RANK, 512).astype(BF16)
    tok_consts = [row(P["mla_cq_g"][l]), row(P["mla_ckv_g"][l]), wuq, qng, jnp.asarray(bd96, BF16),
                  kng, knr, wuk, jnp.asarray(s8, BF16),
                  row(jnp.tile(P["diff_qn_g"][l], 8)), jnp.asarray(bd64, BF16),
                  row(jnp.tile(P["diff_kn_g"][l], 2))]
    wuv = jnp.einsum("chd,hg->hcgd", P["mla_w_uv"][l], jnp.eye(MLA_HEADS, dtype=F32))
    wuv = wuv.reshape(MLA_HEADS * MLA_KV_RANK, 512).astype(BF16)
    return tok_consts, mq, wuv, jnp.asarray(selsb, BF16)


def _in_weights(w):
    d = w.shape[0]
    small = jnp.concatenate([w[:, 0:512], w[:, 544:1056], w[:, 1056:1568], w[:, 1824:2336],
                             w[:, 512:544], jnp.zeros((d, LANE - MLA_D_ROPE), w.dtype),
                             w[:, 1568:1824], w[:, 2336:2464]], axis=1).astype(BF16)
    return small, w[:, 2464:].astype(BF16)


def kernel(x_prompt, x_sample, cache_mla, cache_mla_kscale, cache_diff, cache_sb, state_pool,
           page_table, norm1_g, w_in, mla_cq_g, mla_ckv_g, mla_w_uq, mla_qn_g, mla_kn_g,
           mla_w_uk, mla_w_uv, pool_w, pool_scale, diff_qn_g, diff_kn_g, diff_lambda,
           diff_subln_g, rel_bias, w_branch, w_out, norm2_g, ffn_w_gate, ffn_w_up, ffn_w_down):
    P = dict(mla_cq_g=mla_cq_g, mla_ckv_g=mla_ckv_g, mla_w_uq=mla_w_uq, mla_qn_g=mla_qn_g,
             mla_kn_g=mla_kn_g, mla_w_uk=mla_w_uk, mla_w_uv=mla_w_uv, diff_qn_g=diff_qn_g,
             diff_kn_g=diff_kn_g)
    b, t, d = x_prompt.shape
    bs, ts, _ = x_sample.shape
    depth = w_in.shape[0]
    npages = page_table.shape[1]
    n_past = npages * PAGE
    n_p, n_s = b * t, bs * ts
    row = lambda v: v.reshape(1, -1).astype(F32)

    tq = _pick(t, 128)
    tk = _pick(t, 256)
    npg = _pick(npages, 16)

    cos_p, sin_p = _rope_tables(jnp.arange(t))
    cos_s, sin_s = _rope_tables(n_past + jnp.arange(ts))
    cos_s, sin_s = jnp.tile(cos_s, (bs, 1)), jnp.tile(sin_s, (bs, 1))

    ncase = -(-(tk + 112) // tq)
    ii = jnp.arange(tq)[None, :]
    jj = jnp.arange(tk)[:, None]
    dist_p = jnp.stack([dcase * tq + ii - jj for dcase in range(ncase)] +
                       [jnp.full((tk, tq), REL_MAX_DIST, jnp.int32)])
    bias_p = jnp.moveaxis(_bias_by_distance(rel_bias, dist_p), 0, 1)
    tkc = npg * PAGE
    tt = jnp.arange(ts)[:, None]
    dist_last = n_past + tt - (n_past - tkc + jnp.arange(tkc)[None, :])
    dist_s = jnp.stack([jnp.full((ts, tkc), REL_MAX_DIST + tkc, jnp.int32), dist_last])
    bias_s = jnp.moveaxis(_bias_by_distance(rel_bias, dist_s), 0, 1)
    bias_new = _bias_by_distance(rel_bias, tt - jnp.arange(PAGE)[None, :])

    lmat_p = jnp.asarray(_suffix_matrix(tk).T, BF16)
    umat_s = jnp.asarray(_suffix_matrix(PAGE), BF16)
    uo_s = jnp.asarray(np.concatenate([_suffix_matrix(PAGE), np.ones((PAGE, PAGE), np.float32)], axis=1), BF16)
    e1 = np.zeros((LANE, MLA_HEADS * tq), np.float32)
    for hd in range(MLA_HEADS):
        e1[hd, hd * tq:(hd + 1) * tq] = 1.0
    e3 = jnp.asarray(np.concatenate([e1, e1, e1], axis=0), BF16)

    h = jnp.concatenate([x_prompt.reshape(n_p, d), x_sample.reshape(n_s, d)], axis=0)
    outs = {k: [] for k in ("mla_p", "ks_p", "diff_p", "sb_p", "pool_p",
                            "mla_s", "ks_s", "diff_s", "sb_s", "pool_s")}
    for l in range(depth):
        tok_consts, mq, wuv, selsb = _layer_consts(l, P)
        w_small, w_gates = _in_weights(w_in[l])
        lq1, lk1, lq2, lk2 = diff_lambda[l].astype(F32)
        lam_init = 0.8 - 0.6 * math.exp(-0.3 * l)
        lam = (jnp.exp(jnp.sum(lq1 * lk1)) - jnp.exp(jnp.sum(lq2 * lk2)) + lam_init).reshape(1)
        subln = row(diff_subln_g[l])

        proj = _inproj(h, row(norm1_g[l]), w_small)

        (qm_p, qd_p, qs_p, mrow_p, ks_p, drow_p, km_p, ct_p, ks3_p, kd_p, dvt_p, kb_p, sbt_p) = _tok_prompt(
            proj, b, t, tk, cos_p, sin_p, tok_consts + [mq.T])
        (qm_s, qd_s, qs_s, mrow_s, ks_s, drow_s, km_s, kd_s, kb_s) = _tok_sample(
            proj, n_p, n_s, cos_s, sin_s, tok_consts + [mq])

        pw = pool_w[l].astype(BF16)
        psc = row(pool_scale[l])
        o_pool_p = _pool_prompt(proj, b, t, pw, psc)
        u_s = proj[n_p:, C_POOL:C_DQ].reshape(bs, ts, 512)
        ext_s = jnp.concatenate([state_pool[l], u_s], axis=1)
        o_pool_s = _pool_sample(jnp.swapaxes(ext_s, 0, 1), pw, psc, n_past)
        o_pool_s = jnp.swapaxes(o_pool_s, 0, 1).reshape(n_s, 512)

        g_tile = jnp.broadcast_to(diff_subln_g[l].astype(F32)[:, None], (DIFF_DV, tq))
        o_mla_p = _prompt_attn("mla", qm_p, km_p, ct_p, (ks3_p, e3, wuv.T), b, t, tq, tk)
        o_diff_p = _prompt_attn("diff", qd_p, kd_p, dvt_p, (lam, bias_p, g_tile), b, t, tq, tk,
                                lam_init=lam_init)
        o_sb_p = _prompt_attn("sb", qs_p, kb_p, sbt_p, (lmat_p,), b, t, tq, tk)

        pad_new = lambda a: jnp.pad(a.reshape(bs, ts, -1), ((0, 0), (0, PAGE - ts), (0, 0))).astype(BF16)
        ksnew = jnp.pad(jnp.swapaxes(ks_s.reshape(bs, ts, MLA_HEADS), 1, 2), ((0, 0), (0, 0), (0, PAGE - ts)))
        o_mla_s = _sample_attn("mla", l, page_table, qm_s[0], pad_new(km_s), (ksnew, wuv),
                               (cache_mla, cache_mla_kscale), npg)
        o_diff_s = _sample_attn("diff", l, page_table, qd_s[0], pad_new(kd_s),
                                (lam, bias_s, bias_new, subln, lam_init), (cache_diff,), npg)
        o_sb_s = _sample_attn("sb", l, page_table, qs_s[0], pad_new(kb_s), (umat_s, uo_s, selsb),
                              (cache_sb,), npg)

        cat = lambda a, c: jnp.concatenate([a, c.astype(BF16)], axis=0)
        o_all = jnp.stack([cat(o_mla_p, o_mla_s), cat(o_pool_p, o_pool_s),
                           cat(o_diff_p, o_diff_s), cat(o_sb_p, o_sb_s)])

        merged = _merge(h, row(norm1_g[l]), w_gates, o_all, w_branch[l].astype(BF16))
        h = _outproj(merged, w_out[l].astype(BF16), h)
        h = _ffn(h, row(norm2_g[l]), ffn_w_gate[l].astype(BF16), ffn_w_up[l].astype(BF16),
                 ffn_w_down[l].astype(BF16))

        u_p = proj[:n_p, C_POOL:C_DQ].reshape(b, t, 512)
        outs["mla_p"].append(mrow_p.reshape(b, t, MLA_ROW))
        outs["ks_p"].append(ks_p.reshape(b, t, MLA_HEADS))
        outs["diff_p"].append(drow_p.reshape(b, t, DIFF_ROW))
        outs["sb_p"].append(proj[:n_p, C_SKV:C_END].reshape(b, t, SB_ROW))
        outs["pool_p"].append(jnp.concatenate([jnp.zeros((b, POOL_PAD, 512), F32), u_p], axis=1)[:, -POOL_PAD:])
        outs["mla_s"].append(mrow_s.reshape(bs, ts, MLA_ROW))
        outs["ks_s"].append(ks_s.reshape(bs, ts, MLA_HEADS))
        outs["diff_s"].append(drow_s.reshape(bs, ts, DIFF_ROW))
        outs["sb_s"].append(proj[n_p:, C_SKV:C_END].reshape(bs, ts, SB_ROW))
        outs["pool_s"].append(ext_s[:, -POOL_PAD:])

    st = lambda k: jnp.stack(outs[k])
    return (h[:n_p].reshape(b, t, d), h[n_p:].reshape(bs, ts, d),
            st("mla_p"), st("ks_p"), st("diff_p"), st("sb_p"), st("pool_p"),
            st("mla_s"), st("ks_s"), st("diff_s"), st("sb_s"), st("pool_s"))
```

```python
import functools
import math

import numpy as np
import jax
import jax.numpy as jnp
from jax import lax
from jax.experimental import pallas as pl
from jax.experimental.pallas import tpu as pltpu

F32 = jnp.float32
BF16 = jnp.bfloat16

N_BRANCH = 4
MLA_HEADS = 8
MLA_D_NOPE = 64
MLA_D_ROPE = 32
MLA_D_QK = MLA_D_NOPE + MLA_D_ROPE
MLA_D_V = 64
MLA_KV_RANK = 128
MLA_Q_RANK = 384
MLA_ROW = MLA_KV_RANK + MLA_D_ROPE
ROPE_BASE = 10000.0
POOL_WINDOWS = (2, 4, 8, 16)
POOL_GROUP = 128
POOL_PAD = max(POOL_WINDOWS) - 1
DIFF_HEADS = 4
DIFF_D = 64
DIFF_DV = 128
DIFF_ROW = 2 * DIFF_D + DIFF_DV
SB_HEADS = 8
SB_D = 64
SB_ROW = 2 * SB_D
REL_BUCKETS = 32
REL_MAX_DIST = 128
EPS = 1e-6
PAGE = 128

LANE = 128
MLA_QW = 256
VMEM_LIMIT = 56 * 1024 * 1024

C_CQ, C_CKV, C_POOL, C_DQ, C_SQ = 0, 384, 512, 1024, 1536
C_KR, C_DK, C_DV, C_SKV, C_END = 2048, 2176, 2304, 2432, 2560


def _dot(a, b):
    return jnp.dot(a, b, preferred_element_type=F32)


def _dot_nt(a, b):
    return lax.dot_general(a, b, (((1,), (1,)), ((), ())), preferred_element_type=F32)


def _dot_hilo(x, m):
    hi = x.astype(BF16)
    lo = (x - hi.astype(F32)).astype(BF16)
    return _dot(hi, m) + _dot(lo, m)


def _rms(x, g):
    return x * lax.rsqrt(jnp.mean(x * x, axis=-1, keepdims=True) + EPS) * g


def _pick(n, pref):
    t = min(n, pref)
    while n % t:
        t //= 2
    return t


def _cparams(sem):
    return pltpu.CompilerParams(dimension_semantics=sem, vmem_limit_bytes=VMEM_LIMIT)


def _inproj_kernel(x_ref, g_ref, w_ref, o_ref, xn_sc):
    @pl.when(pl.program_id(1) == 0)
    def _():
        xn_sc[...] = _rms(x_ref[...], g_ref[...]).astype(BF16)

    o_ref[...] = _dot(xn_sc[...], w_ref[...])


def _inproj(h, g, w):
    n, d = h.shape
    nout = w.shape[1]
    tm = _pick(n, 512)
    tn = _pick(nout, 512)
    return pl.pallas_call(
        _inproj_kernel,
        out_shape=jax.ShapeDtypeStruct((n, nout), F32),
        grid=(n // tm, nout // tn),
        in_specs=[pl.BlockSpec((tm, d), lambda i, j: (i, 0)),
                  pl.BlockSpec((1, d), lambda i, j: (0, 0)),
                  pl.BlockSpec((d, tn), lambda i, j: (0, j))],
        out_specs=pl.BlockSpec((tm, tn), lambda i, j: (i, j)),
        scratch_shapes=[pltpu.VMEM((tm, d), BF16)],
        compiler_params=_cparams(("arbitrary", "arbitrary")),
        name="inproj",
    )(h, g, w)


def _tok_math(p_ref, cos_ref, sin_ref, cqg, ckvg, wuq, qng, bd96, kng, knr, wuk, s8, dqg, bd64, dkg):
    tm = p_ref.shape[0]
    lane = lax.broadcasted_iota(jnp.int32, (tm, LANE), 1)
    cos = cos_ref[...]
    sin = sin_ref[...]

    cqn = _rms(p_ref[:, C_CQ:C_CKV], cqg[...]).astype(BF16)
    q = _dot(cqn, wuq[...])
    ss = _dot_hilo(q * q, bd96[...])
    qn = q * lax.rsqrt(ss * (1.0 / MLA_D_QK) + EPS) * qng[...]
    qnope = qn[:, :512] * kng[...]
    x1 = qn[:, 512:640]
    x2 = qn[:, 640:768]
    qall = jnp.concatenate([qnope, x1 * cos - x2 * sin, x1 * sin + x2 * cos], axis=1).astype(BF16)

    ckvn = _rms(p_ref[:, C_CKV:C_POOL], ckvg[...])
    knope = _dot(ckvn.astype(BF16), wuk[...])
    ssq = _dot_hilo(knope * knope, s8[...])
    kr = p_ref[:, C_KR:C_DK]
    krsq = jnp.sum(kr * kr, axis=-1, keepdims=True)
    kscale = lax.rsqrt((ssq + krsq) * (1.0 / MLA_D_QK) + EPS)
    krk = kr * knr[...]
    half = MLA_D_ROPE // 2
    rot = jnp.where(lane < half, -pltpu.roll(krk, LANE - half, 1), pltpu.roll(krk, half, 1))
    krrot = jnp.where(lane < MLA_D_ROPE, krk * cos + rot * sin, 0.0)

    dq = p_ref[:, C_DQ:C_SQ]
    ssd = _dot_hilo(dq * dq, bd64[...])
    dqn = dq * lax.rsqrt(ssd * (1.0 / DIFF_D) + EPS) * dqg[...]
    lo_half = lane < DIFF_D
    dk = p_ref[:, C_DK:C_DV]
    dk2 = dk * dk
    s_lo = jnp.sum(jnp.where(lo_half, dk2, 0.0), axis=-1, keepdims=True)
    s_hi = jnp.sum(jnp.where(lo_half, 0.0, dk2), axis=-1, keepdims=True)
    dkn = dk * lax.rsqrt(jnp.where(lo_half, s_lo, s_hi) * (1.0 / DIFF_D) + EPS) * dkg[...]
    return dict(qall=qall, ckvn=ckvn, kscale=kscale, krrot=krrot, dqn=dqn, dkn=dkn,
                dv=p_ref[:, C_DV:C_SKV], lane=lane, lo_half=lo_half)


def _tok_sample_kernel(p_ref, cos_ref, sin_ref, cqg, ckvg, wuq, qng, bd96, kng, knr, wuk, s8,
                       dqg, bd64, dkg, mq,
                       qmla_ref, qdiff_ref, qsb_ref, mrow_ref, ks_ref, drow_ref,
                       kmla_ref, kdiff_ref, ksb_ref):
    v = _tok_math(p_ref, cos_ref, sin_ref, cqg, ckvg, wuq, qng, bd96, kng, knr, wuk, s8, dqg, bd64, dkg)
    lo_half = v["lo_half"]
    qcat = _dot(v["qall"], mq[...])
    for h in range(MLA_HEADS):
        qmla_ref[0, h] = qcat[:, h * MLA_QW:(h + 1) * MLA_QW]
    ks_ref[...] = v["kscale"][:, :MLA_HEADS]
    mrow_ref[:, :MLA_KV_RANK] = v["ckvn"]
    mrow_ref[:, MLA_KV_RANK:] = v["krrot"][:, :MLA_D_ROPE]
    kmla_ref[:, :LANE] = v["ckvn"]
    kmla_ref[:, LANE:] = v["krrot"]
    for h in range(DIFF_HEADS):
        slab = v["dqn"][:, h * LANE:(h + 1) * LANE]
        qdiff_ref[0, 2 * h] = jnp.where(lo_half, slab, 0.0)
        qdiff_ref[0, 2 * h + 1] = jnp.where(lo_half, 0.0, slab)
    drow_ref[:, :LANE] = v["dkn"]
    drow_ref[:, LANE:] = v["dv"]
    kdiff_ref[:, :LANE] = v["dkn"]
    kdiff_ref[:, LANE:] = v["dv"]
    for hh in range(SB_HEADS // 2):
        slab = p_ref[:, C_SQ + hh * LANE:C_SQ + (hh + 1) * LANE]
        qsb_ref[0, 2 * hh] = jnp.where(lo_half, slab, 0.0)
        qsb_ref[0, 2 * hh + 1] = jnp.where(lo_half, pltpu.roll(slab, SB_D, 1), 0.0)
    ksb_ref[...] = p_ref[:, C_SKV:C_END]


def _tok_prompt_kernel(p_ref, cos_ref, sin_ref, cqg, ckvg, wuq, qng, bd96, kng, knr, wuk, s8,
                       dqg, bd64, dkg, mqt,
                       qmla_ref, qdiff_ref, qsb_ref, mrow_ref, ks_ref, drow_ref,
                       kmla_ref, ct_ref, ks3_ref, kdiff_ref, dvt_ref, ksb_ref, sbt_ref):
    tm = p_ref.shape[0]
    v = _tok_math(p_ref, cos_ref, sin_ref, cqg, ckvg, wuq, qng, bd96, kng, knr, wuk, s8, dqg, bd64, dkg)
    lane = v["lane"]
    row = lax.broadcasted_iota(jnp.int32, (LANE, tm), 0)
    top = row < DIFF_D
    qcat_t = _dot_nt(mqt[...], v["qall"])
    for h in range(MLA_HEADS):
        qmla_ref[0, h] = qcat_t[h * MLA_QW:(h + 1) * MLA_QW].astype(BF16)
    ks_ref[...] = v["kscale"][:, :MLA_HEADS]
    mrow_ref[:, :MLA_KV_RANK] = v["ckvn"]
    mrow_ref[:, MLA_KV_RANK:] = v["krrot"][:, :MLA_D_ROPE]
    kmla_ref[:, :LANE] = v["ckvn"].astype(BF16)
    kmla_ref[:, LANE:] = v["krrot"].astype(BF16)
    ct_ref[0, 0] = v["ckvn"].T.astype(BF16)
    ksm = jnp.where(lane < MLA_HEADS, v["kscale"] * (MLA_D_QK ** -0.5), 0.0)
    hi = ksm.astype(BF16)
    r1 = ksm - hi.astype(F32)
    mid = r1.astype(BF16)
    ks3_ref[:, :LANE] = hi
    ks3_ref[:, LANE:2 * LANE] = mid
    ks3_ref[:, 2 * LANE:] = (r1 - mid.astype(F32)).astype(BF16)
    for h in range(DIFF_HEADS):
        slab_t = (v["dqn"][:, h * LANE:(h + 1) * LANE] * (DIFF_D ** -0.5)).T
        qdiff_ref[0, 2 * h] = jnp.where(top, slab_t, 0.0).astype(BF16)
        qdiff_ref[0, 2 * h + 1] = jnp.where(top, 0.0, slab_t).astype(BF16)
    drow_ref[:, :LANE] = v["dkn"]
    drow_ref[:, LANE:] = v["dv"]
    kdiff_ref[...] = v["dkn"].astype(BF16)
    dvt_ref[0, 0] = v["dv"].T.astype(BF16)
    for hh in range(SB_HEADS // 2):
        slab_t = (p_ref[:, C_SQ + hh * LANE:C_SQ + (hh + 1) * LANE] * (SB_D ** -0.5)).T
        qsb_ref[0, 2 * hh] = jnp.where(top, slab_t, 0.0).astype(BF16)
        qsb_ref[0, 2 * hh + 1] = jnp.concatenate(
            [slab_t[SB_D:], jnp.zeros((LANE - SB_D, tm), F32)], axis=0).astype(BF16)
    skv = p_ref[:, C_SKV:C_END]
    ksb_ref[...] = skv.astype(BF16)
    sbt_ref[0, 0] = skv.T.astype(BF16)


def _tok_sample(proj, row0, n, cos, sin, consts):
    tm = _pick(n, 256)
    rb0 = row0 // tm
    assert row0 % tm == 0
    full = lambda a: pl.BlockSpec(a.shape, lambda i: (0,) * a.ndim)
    rows = lambda w: pl.BlockSpec((tm, w), lambda i: (i, 0))
    qspec = lambda w: pl.BlockSpec((1, 8, tm, w), lambda i: (0, 0, i, 0))
    out_shape = [
        jax.ShapeDtypeStruct((1, 8, n, MLA_QW), F32),
        jax.ShapeDtypeStruct((1, 8, n, LANE), F32),
        jax.ShapeDtypeStruct((1, 8, n, LANE), F32),
        jax.ShapeDtypeStruct((n, MLA_ROW), F32),
        jax.ShapeDtypeStruct((n, MLA_HEADS), F32),
        jax.ShapeDtypeStruct((n, DIFF_ROW), F32),
        jax.ShapeDtypeStruct((n, MLA_QW), F32),
        jax.ShapeDtypeStruct((n, DIFF_ROW), F32),
        jax.ShapeDtypeStruct((n, SB_ROW), F32),
    ]
    out_specs = [qspec(MLA_QW), qspec(LANE), qspec(LANE), rows(MLA_ROW), rows(MLA_HEADS),
                 rows(DIFF_ROW), rows(MLA_QW), rows(DIFF_ROW), rows(SB_ROW)]
    in_specs = [pl.BlockSpec((tm, C_END), lambda i: (rb0 + i, 0)), rows(LANE), rows(LANE)]
    in_specs += [full(c) for c in consts]
    return pl.pallas_call(
        _tok_sample_kernel, out_shape=out_shape, grid=(n // tm,), in_specs=in_specs, out_specs=out_specs,
        compiler_params=_cparams(("arbitrary",)), name="tok_sample",
    )(proj, cos, sin, *consts)


def _tok_prompt(proj, b, t, tm, cos, sin, consts):
    n = b * t
    nt = t // tm
    full = lambda a: pl.BlockSpec(a.shape, lambda bi, i: (0,) * a.ndim)
    rows = lambda w: pl.BlockSpec((tm, w), lambda bi, i: (bi * nt + i, 0))
    qspec = lambda w: pl.BlockSpec((1, 8, w, tm), lambda bi, i: (bi, 0, 0, i))
    tspec = pl.BlockSpec((1, 1, LANE, tm), lambda bi, i: (bi, i, 0, 0))
    tshape = jax.ShapeDtypeStruct((b, nt, LANE, tm), BF16)
    out_shape = [
        jax.ShapeDtypeStruct((b, 8, MLA_QW, t), BF16),
        jax.ShapeDtypeStruct((b, 8, LANE, t), BF16),
        jax.ShapeDtypeStruct((b, 8, LANE, t), BF16),
        jax.ShapeDtypeStruct((n, MLA_ROW), F32),
        jax.ShapeDtypeStruct((n, MLA_HEADS), F32),
        jax.ShapeDtypeStruct((n, DIFF_ROW), F32),
        jax.ShapeDtypeStruct((n, MLA_QW), BF16), tshape,
        jax.ShapeDtypeStruct((n, 3 * LANE), BF16),
        jax.ShapeDtypeStruct((n, LANE), BF16), tshape,
        jax.ShapeDtypeStruct((n, SB_ROW), BF16), tshape,
    ]
    out_specs = [qspec(MLA_QW), qspec(LANE), qspec(LANE), rows(MLA_ROW), rows(MLA_HEADS), rows(DIFF_ROW),
                 rows(MLA_QW), tspec, rows(3 * LANE), rows(LANE), tspec, rows(SB_ROW), tspec]
    in_specs = [pl.BlockSpec((tm, C_END), lambda bi, i: (bi * nt + i, 0)),
                pl.BlockSpec((tm, LANE), lambda bi, i: (i, 0)),
                pl.BlockSpec((tm, LANE), lambda bi, i: (i, 0))] + [full(c) for c in consts]
    return pl.pallas_call(
        _tok_prompt_kernel, out_shape=out_shape, grid=(b, nt), in_specs=in_specs, out_specs=out_specs,
        compiler_params=_cparams(("arbitrary", "arbitrary")), name="tok_prompt",
    )(proj, cos, sin, *consts)


def _pool_prompt_kernel(u_ref, halo_ref, w_ref, sc_ref, o_ref, ext_sc):
    tm = u_ref.shape[0]
    ti = pl.program_id(1)
    u = u_ref[...]
    ext_sc[16:, :] = u
    ext_sc[:16, :] = jnp.where(ti == 0, 0.0, halo_ref[...])
    pos = ti * tm + lax.broadcasted_iota(jnp.int32, (tm, 1), 0)
    outs = []
    for g, w in enumerate(POOL_WINDOWS):
        c0 = g * POOL_GROUP
        wsum = u[:, c0:c0 + POOL_GROUP]
        for d in range(1, w):
            wsum = wsum + ext_sc[16 - d:16 - d + tm, c0:c0 + POOL_GROUP]
        cnt = jnp.minimum(pos + 1, w).astype(F32)
        pooled = wsum / cnt - u[:, c0:c0 + POOL_GROUP]
        outs.append(_dot(pooled.astype(BF16), w_ref[g]))
    o_ref[...] = (jnp.concatenate(outs, axis=1) * sc_ref[...]).astype(o_ref.dtype)


def _pool_prompt(proj, b, t, w, scale):
    tm = _pick(t, 512)
    nt = t // tm
    cb = C_POOL // 512
    return pl.pallas_call(
        _pool_prompt_kernel,
        out_shape=jax.ShapeDtypeStruct((b * t, 512), BF16),
        grid=(b, nt),
        in_specs=[pl.BlockSpec((tm, 512), lambda bi, i: (bi * nt + i, cb)),
                  pl.BlockSpec((16, 512), lambda bi, i: (jnp.maximum((bi * nt + i) * (tm // 16) - 1, 0), cb)),
                  pl.BlockSpec(w.shape, lambda bi, i: (0, 0, 0)),
                  pl.BlockSpec(scale.shape, lambda bi, i: (0, 0))],
        out_specs=pl.BlockSpec((tm, 512), lambda bi, i: (bi * nt + i, 0)),
        scratch_shapes=[pltpu.VMEM((tm + 16, 512), F32)],
        compiler_params=_cparams(("arbitrary", "arbitrary")),
        name="pool_prompt",
    )(proj, proj, w, scale)


def _pool_sample_kernel(ext_ref, w_ref, sc_ref, o_ref, *, n_past):
    ts = o_ref.shape[0]
    for t in range(ts):
        outs = []
        for g, w in enumerate(POOL_WINDOWS):
            c0 = g * POOL_GROUP
            cur = ext_ref[POOL_PAD + t, :, c0:c0 + POOL_GROUP]
            wsum = cur
            for d in range(1, w):
                wsum = wsum + ext_ref[POOL_PAD + t - d, :, c0:c0 + POOL_GROUP]
            cnt = float(min(n_past + t + 1, w))
            pooled = wsum / cnt - cur
            outs.append(_dot(pooled.astype(BF16), w_ref[g]))
        o_ref[t] = jnp.concatenate(outs, axis=1) * sc_ref[...]


def _pool_sample(ext, w, scale, n_past):
    rows, seqs, _ = ext.shape
    ts = rows - POOL_PAD
    return pl.pallas_call(
        functools.partial(_pool_sample_kernel, n_past=n_past),
        out_shape=jax.ShapeDtypeStruct((ts, seqs, 512), F32),
        grid=(1,),
        in_specs=[pl.BlockSpec(ext.shape, lambda i: (0, 0, 0)),
                  pl.BlockSpec(w.shape, lambda i: (0, 0, 0)),
                  pl.BlockSpec(scale.shape, lambda i: (0, 0))],
        out_specs=pl.BlockSpec((ts, seqs, 512), lambda i: (0, 0, 0)),
        compiler_params=_cparams(("arbitrary",)),
        name="pool_sample",
    )(ext, w, scale)


def _col_qpos(qi, tq, ncol):
    return qi * tq + (lax.broadcasted_iota(jnp.int32, (1, ncol), 1) & (tq - 1))


def _softmax_step_t(s, vt, m_sc, l_sc, acc_sc):
    m_old = m_sc[...]
    m_new = jnp.maximum(m_old, jnp.max(s, axis=0, keepdims=True))
    alpha = jnp.exp(m_old - m_new)
    p = jnp.exp(s - m_new)
    l_sc[...] = alpha * l_sc[...] + jnp.sum(p, axis=0, keepdims=True)
    acc_sc[...] = alpha * acc_sc[...] + _dot(vt, p.astype(BF16))
    m_sc[...] = m_new


def _mla_prompt_kernel(qt_ref, k_ref, ct_ref, ks3_ref, e3_ref, wuvt_ref, o_ref, m_sc, l_sc, acc_sc, *, tq, tk):
    qi = pl.program_id(1)
    ncol = MLA_HEADS * tq
    qt = jnp.concatenate([qt_ref[0, h] for h in range(MLA_HEADS)], axis=1)
    m_sc[...] = jnp.full(m_sc.shape, -jnp.inf, F32)
    l_sc[...] = jnp.zeros(l_sc.shape, F32)
    acc_sc[...] = jnp.zeros(acc_sc.shape, F32)
    nkb = ((qi + 1) * tq + tk - 1) // tk
    qpos = _col_qpos(qi, tq, ncol)

    def step(kb, masked):
        start = pl.multiple_of(kb * tk, tk)
        s = _dot(k_ref[0, pl.ds(start, tk), :], qt)
        s = s * _dot(ks3_ref[0, pl.ds(start, tk), :], e3_ref[...])
        if masked:
            kpos = kb * tk + lax.broadcasted_iota(jnp.int32, (tk, 1), 0)
            s = jnp.where(kpos <= qpos, s, -jnp.inf)
        _softmax_step_t(s, ct_ref[0, kb], m_sc, l_sc, acc_sc)

    def body(kb, carry):
        step(kb, False)
        return carry

    lax.fori_loop(0, nkb - 1, body, 0)
    step(nkb - 1, True)
    o = acc_sc[...] / l_sc[...]
    ocat = jnp.concatenate([o[:, h * tq:(h + 1) * tq] for h in range(MLA_HEADS)], axis=0)
    o_ref[...] = _dot(wuvt_ref[...], ocat.astype(BF16)).T.astype(o_ref.dtype)


def _diff_prompt_kernel(lam_ref, qt_ref, k_ref, vt_ref, bias_ref, g_ref, o_ref, m_sc, l_sc, acc_sc,
                        *, tq, tk, lam_init):
    qi = pl.program_id(1)
    nh = 2 * DIFF_HEADS
    ncol = nh * tq
    qt = jnp.concatenate([qt_ref[0, j] for j in range(nh)], axis=1)
    m_sc[...] = jnp.full(m_sc.shape, -jnp.inf, F32)
    l_sc[...] = jnp.zeros(l_sc.shape, F32)
    acc_sc[...] = jnp.zeros(acc_sc.shape, F32)
    nkb = ((qi + 1) * tq + tk - 1) // tk
    ntile = bias_ref.shape[0]
    qpos = _col_qpos(qi, tq, ncol)

    def step(kb, masked):
        start = pl.multiple_of(kb * tk, tk)
        s = _dot(k_ref[0, pl.ds(start, tk), :], qt)
        bt = bias_ref[jnp.minimum((qi * tq - kb * tk) // tq, ntile - 1)]
        s = s + jnp.concatenate([bt[h] for h in range(DIFF_HEADS) for _ in range(2)], axis=1)
        if masked:
            kpos = kb * tk + lax.broadcasted_iota(jnp.int32, (tk, 1), 0)
            s = jnp.where(kpos <= qpos, s, -jnp.inf)
        _softmax_step_t(s, vt_ref[0, kb], m_sc, l_sc, acc_sc)

    def body(kb, carry):
        step(kb, False)
        return carry

    lax.fori_loop(0, nkb - 1, body, 0)
    step(nkb - 1, True)
    o = acc_sc[...] / l_sc[...]
    lam = lam_ref[0]
    outs = []
    for h in range(DIFF_HEADS):
        oh = o[:, (2 * h) * tq:(2 * h + 1) * tq] - lam * o[:, (2 * h + 1) * tq:(2 * h + 2) * tq]
        oh = oh * lax.rsqrt(jnp.mean(oh * oh, axis=0, keepdims=True) + EPS) * g_ref[...]
        outs.append(oh * (1.0 - lam_init))
    o_ref[...] = jnp.concatenate(outs, axis=0).T.astype(o_ref.dtype)


def _softplus(z):
    return jnp.maximum(z, 0.0) + jnp.log1p(jnp.exp(-jnp.abs(z)))


SB_LOG_FLOOR = -104.0


def _sb_prompt_kernel(qt_ref, k_ref, vt_ref, lmat_ref, o_ref, r_sc, acc_sc, *, tq, tk):
    qi = pl.program_id(1)
    ncol = SB_HEADS * tq
    qt = jnp.concatenate([qt_ref[0, h] for h in range(SB_HEADS)], axis=1)
    r_sc[...] = jnp.zeros(r_sc.shape, F32)
    acc_sc[...] = jnp.zeros(acc_sc.shape, F32)
    nkb = ((qi + 1) * tq + tk - 1) // tk
    qpos = _col_qpos(qi, tq, ncol)

    def step(kb, masked):
        start = pl.multiple_of(kb * tk, tk)
        z = _dot(k_ref[0, pl.ds(start, tk), :], qt)
        sp = _softplus(z)
        lk = -sp
        if masked:
            vis = kb * tk + lax.broadcasted_iota(jnp.int32, (tk, 1), 0) < qpos
            lk = jnp.where(vis, lk, 0.0)
        hi = lk.astype(BF16)
        lo = (lk - hi.astype(F32)).astype(BF16)
        suf = _dot(lmat_ref[...], hi) + _dot(lmat_ref[...], lo)
        a = jnp.exp((z - sp) + suf + r_sc[...])
        if masked:
            a = jnp.where(vis, a, 0.0)
        acc_sc[...] += _dot(vt_ref[0, kb, SB_D:, :], a.astype(BF16))
        r_sc[...] += jnp.sum(lk, axis=0, keepdims=True)

    step(nkb - 1, True)

    def more(it):
        return jnp.logical_and(it < nkb - 1, jnp.max(r_sc[...]) > SB_LOG_FLOOR)

    def body(it):
        step(nkb - 2 - it, False)
        return it + 1

    lax.while_loop(more, body, jnp.int32(0))
    acc = acc_sc[...]
    ocat = jnp.concatenate([acc[:, h * tq:(h + 1) * tq] for h in range(SB_HEADS)], axis=0)
    o_ref[...] = ocat.T.astype(o_ref.dtype)


def _prompt_attn(kind, qt, krows, vt, extra, b, t, tq, tk, **kw):
    nq = t // tq
    w = qt.shape[2]
    kw_ = krows.shape[-1]
    k3 = krows.reshape(b, t, kw_)
    qspec = pl.BlockSpec((1, 8, w, tq), lambda bi, i, *_: (bi, 0, 0, i))
    kspec = pl.BlockSpec((1, t, kw_), lambda bi, i, *_: (bi, 0, 0))
    vspec = pl.BlockSpec((1,) + vt.shape[1:], lambda bi, i, *_: (bi, 0, 0, 0))
    ospec = pl.BlockSpec((tq, 512), lambda bi, i, *_: (bi * nq + i, 0))
    full = lambda a: pl.BlockSpec(a.shape, lambda bi, i, *_: (0,) * a.ndim)
    ncol = 8 * tq
    out_shape = jax.ShapeDtypeStruct((b * t, 512), BF16)
    stats = [pltpu.VMEM((1, ncol), F32), pltpu.VMEM((1, ncol), F32), pltpu.VMEM((LANE, ncol), F32)]
    if kind == "mla":
        ks3, e3, wuvt = extra
        ks3 = ks3.reshape(b, t, 3 * LANE)
        return pl.pallas_call(
            functools.partial(_mla_prompt_kernel, tq=tq, tk=tk),
            out_shape=out_shape, grid=(b, nq),
            in_specs=[qspec, kspec, vspec, pl.BlockSpec((1, t, 3 * LANE), lambda bi, i: (bi, 0, 0)),
                      full(e3), full(wuvt)],
            out_specs=ospec, scratch_shapes=stats,
            compiler_params=_cparams(("arbitrary", "arbitrary")), name="mla_prompt",
        )(qt, k3, vt, ks3, e3, wuvt)
    if kind == "diff":
        lam, bias, g = extra
        return pl.pallas_call(
            functools.partial(_diff_prompt_kernel, tq=tq, tk=tk, lam_init=kw["lam_init"]),
            out_shape=out_shape,
            grid_spec=pltpu.PrefetchScalarGridSpec(
                num_scalar_prefetch=1, grid=(b, nq),
                in_specs=[qspec, kspec, vspec, full(bias), full(g)],
                out_specs=ospec, scratch_shapes=stats),
            compiler_params=_cparams(("arbitrary", "arbitrary")), name="diff_prompt",
        )(lam, qt, k3, vt, bias, g)
    (lmat,) = extra
    return pl.pallas_call(
        functools.partial(_sb_prompt_kernel, tq=tq, tk=tk),
        out_shape=out_shape, grid=(b, nq),
        in_specs=[qspec, kspec, vspec, full(lmat)],
        out_specs=ospec,
        scratch_shapes=[pltpu.VMEM((1, ncol), F32), pltpu.VMEM((SB_D, ncol), F32)],
        compiler_params=_cparams(("arbitrary", "arbitrary")), name="sb_prompt",
    )(qt, k3, vt, lmat)


def _page_copies(cache, layer, pt_ref, seq, chunk, buf, sem, slot, npg):
    return [pltpu.make_async_copy(cache.at[layer, pt_ref[seq, chunk * npg + i]],
                                  buf.at[slot, i], sem.at[slot]) for i in range(npg)]


def _paged_loop(caches, bufs, sems, layer, pt_ref, nseq, nch, npg, compute, reverse):
    seq = pl.program_id(0)

    def chunk_of(c):
        return nch - 1 - c if reverse else c

    def start(sq, c, slot):
        for cache, buf, sem in zip(caches, bufs, sems):
            for cp in _page_copies(cache, layer, pt_ref, sq, chunk_of(c), buf, sem, slot, npg):
                cp.start()

    def wait(sq, c, slot):
        for cache, buf, sem in zip(caches, bufs, sems):
            for cp in _page_copies(cache, layer, pt_ref, sq, chunk_of(c), buf, sem, slot, npg):
                cp.wait()

    @pl.when(seq == 0)
    def _():
        start(seq, 0, 0)

    def body(c, carry):
        slot = (seq * nch + c) % 2
        last = c == nch - 1
        nseq_ = jnp.where(last, seq + 1, seq)
        nc = jnp.where(last, 0, c + 1)

        @pl.when(nseq_ < nseq)
        def _():
            start(nseq_, nc, 1 - slot)

        wait(seq, c, slot)
        return compute(chunk_of(c), slot, carry)

    return body


def _sb_block(z, mask, v, umat, r):
    sp = _softplus(z)
    lk = -sp
    if mask is not None:
        shape3 = (-1,) + mask.shape[1:]
        lk = jnp.where(mask, lk.reshape(shape3), 0.0).reshape(z.shape)
    suf = _dot_hilo(lk, umat)
    a = jnp.exp((z - sp) + suf + r)
    if mask is not None:
        a = jnp.where(mask, a.reshape(shape3), 0.0).reshape(z.shape)
    return _dot(a.astype(BF16), v), r + jnp.sum(lk, axis=-1, keepdims=True)


def _mla_sample_kernel(pt_ref, q_ref, knew_ref, ksnew_ref, kspast_ref, wuv_ref, rows_hbm, o_ref,
                       rbuf, rsem, *, layer, nseq, nch, npg):
    ts = q_ref.shape[1]
    nr = MLA_HEADS * ts
    tk = npg * PAGE
    q = q_ref[...].reshape(nr, MLA_QW).astype(BF16)
    qc = q[:, :MLA_KV_RANK]
    qr = q[:, MLA_KV_RANK:]
    scale = MLA_D_QK ** -0.5

    def update(s, v, carry, v_transposed=False):
        m, l, acc = carry
        m_new = jnp.maximum(m, jnp.max(s, axis=-1, keepdims=True))
        alpha = jnp.exp(m - m_new)
        p = jnp.exp(s - m_new)
        pv = _dot_nt(p.astype(BF16), v) if v_transposed else _dot(p.astype(BF16), v)
        return (m_new, alpha * l + jnp.sum(p, axis=-1, keepdims=True), alpha * acc + pv)

    def compute(chunk, slot, carry):
        pages = [rbuf[slot, i].astype(BF16) for i in range(npg)]
        ct = jnp.concatenate([pg[:MLA_KV_RANK] for pg in pages], axis=1)
        krt = jnp.concatenate([pg[MLA_KV_RANK:] for pg in pages], axis=1)
        krt = jnp.concatenate([krt, jnp.zeros((LANE - MLA_D_ROPE, tk), BF16)], axis=0)
        s = (_dot(qc, ct) + _dot(qr, krt)).reshape(MLA_HEADS, ts, tk)
        ks = kspast_ref[0, chunk] * scale
        s = (s * ks[:, None, :]).reshape(nr, tk)
        return update(s, ct, carry, v_transposed=True)

    body = _paged_loop((rows_hbm,), (rbuf,), (rsem,), layer, pt_ref, nseq, nch, npg, compute, reverse=False)
    init = (jnp.full((nr, 1), -jnp.inf, F32), jnp.zeros((nr, 1), F32), jnp.zeros((nr, MLA_KV_RANK), F32))
    carry = lax.fori_loop(0, nch, body, init)

    kn = knew_ref[0]
    s = _dot_nt(q, kn).reshape(MLA_HEADS, ts, PAGE)
    s = s * (ksnew_ref[0] * scale)[:, None, :]
    t_i = lax.broadcasted_iota(jnp.int32, (1, ts, PAGE), 1)
    j_i = lax.broadcasted_iota(jnp.int32, (1, ts, PAGE), 2)
    s = jnp.where(j_i <= t_i, s, -jnp.inf).reshape(nr, PAGE)
    m, l, acc = update(s, kn[:, :MLA_KV_RANK], carry)
    o = acc / l
    ocat = jnp.concatenate([o[h * ts:(h + 1) * ts] for h in range(MLA_HEADS)], axis=1)
    o_ref[...] = _dot(ocat.astype(BF16), wuv_ref[...])


def _diff_sample_kernel(pt_ref, lam_ref, q_ref, knew_ref, bias_ref, bnew_ref, g_ref, rows_hbm, o_ref,
                        rbuf, rsem, *, layer, nseq, nch, npg, lam_init):
    ts = q_ref.shape[1]
    nh = 2 * DIFF_HEADS
    nr = nh * ts
    tk = npg * PAGE
    q = q_ref[...].reshape(nr, LANE).astype(BF16)

    def update(s, v, carry):
        m, l, acc = carry
        m_new = jnp.maximum(m, jnp.max(s, axis=-1, keepdims=True))
        alpha = jnp.exp(m - m_new)
        p = jnp.exp(s - m_new)
        return (m_new, alpha * l + jnp.sum(p, axis=-1, keepdims=True),
                alpha * acc + _dot(p.astype(BF16), v))

    def compute(chunk, slot, carry):
        rows = rbuf[slot].reshape(tk, DIFF_ROW).astype(BF16)
        s = _dot_nt(q, rows[:, :LANE]).reshape(DIFF_HEADS, 2, ts, tk) * (DIFF_D ** -0.5)
        case = jnp.where(chunk == nch - 1, 1, 0)
        s = (s + bias_ref[case][:, None, :, :]).reshape(nr, tk)
        return update(s, rows[:, LANE:], carry)

    body = _paged_loop((rows_hbm,), (rbuf,), (rsem,), layer, pt_ref, nseq, nch, npg, compute, reverse=False)
    init = (jnp.full((nr, 1), -jnp.inf, F32), jnp.zeros((nr, 1), F32), jnp.zeros((nr, DIFF_DV), F32))
    carry = lax.fori_loop(0, nch, body, init)

    kn = knew_ref[0]
    s = _dot_nt(q, kn[:, :LANE]).reshape(DIFF_HEADS, 2, ts, PAGE) * (DIFF_D ** -0.5)
    s = s + bnew_ref[...][:, None, :, :]
    t_i = lax.broadcasted_iota(jnp.int32, (1, 1, ts, PAGE), 2)
    j_i = lax.broadcasted_iota(jnp.int32, (1, 1, ts, PAGE), 3)
    s = jnp.where(j_i <= t_i, s, -jnp.inf).reshape(nr, PAGE)
    m, l, acc = update(s, kn[:, LANE:], carry)
    o = acc / l
    lam = lam_ref[0]
    outs = []
    for h in range(DIFF_HEADS):
        oh = o[(2 * h) * ts:(2 * h + 1) * ts] - lam * o[(2 * h + 1) * ts:(2 * h + 2) * ts]
        outs.append(_rms(oh, g_ref[...]) * (1.0 - lam_init))
    o_ref[...] = jnp.concatenate(outs, axis=1)


def _sb_sample_kernel(pt_ref, q_ref, knew_ref, u_ref, uo_ref, sel_ref, rows_hbm, o_ref, rbuf, rsem,
                      *, layer, nseq, nch, npg):
    ts = q_ref.shape[1]
    nr = SB_HEADS * ts
    tk = npg * PAGE
    q = q_ref[...].reshape(nr, LANE).astype(BF16)

    kn = knew_ref[0]
    z = _dot_nt(q, kn) * (SB_D ** -0.5)
    t_i = lax.broadcasted_iota(jnp.int32, (1, ts, PAGE), 1)
    j_i = lax.broadcasted_iota(jnp.int32, (1, ts, PAGE), 2)
    acc0, r0 = _sb_block(z, j_i < t_i, kn, u_ref[...], jnp.zeros((nr, 1), F32))

    def compute(slot, acc, roff):
        rows = rbuf[slot].reshape(tk, SB_ROW).astype(BF16)
        z = _dot_nt(q, rows) * (SB_D ** -0.5)
        sp = _softplus(z)
        lk = -sp
        hi = lk.astype(BF16)
        lo = (lk - hi.astype(F32)).astype(BF16)
        sufs = [None] * npg
        for i in reversed(range(npg)):
            sl = slice(i * PAGE, (i + 1) * PAGE)
            both = _dot(hi[:, sl], uo_ref[...]) + _dot(lo[:, sl], uo_ref[...])
            sufs[i] = both[:, :PAGE] + roff
            roff = roff + both[:, PAGE:]
        a = jnp.exp((z - sp) + jnp.concatenate(sufs, axis=1))
        return acc + _dot(a.astype(BF16), rows), roff

    seq = pl.program_id(0)

    def copies(sq, c, slot):
        return _page_copies(rows_hbm, layer, pt_ref, sq, nch - 1 - c, rbuf, rsem, slot, npg)

    def needed(c, roff):
        return jnp.logical_and(c < nch, jnp.max(roff) > SB_LOG_FLOOR)

    def visit(carry):
        c, acc, roff = carry
        slot = c % 2
        for cp in copies(seq, c, slot):
            cp.wait()
        acc, roff = compute(slot, acc, roff)

        @pl.when(needed(c + 1, roff))
        def _():
            for cp in copies(seq, c + 1, 1 - slot):
                cp.start()

        return c + 1, acc, roff

    @pl.when(seq == 0)
    def _():
        for cp in copies(seq, 0, 0):
            cp.start()

    carry = visit((jnp.int32(0), acc0, jnp.broadcast_to(r0, (nr, PAGE))))
    _, acc, _ = lax.while_loop(lambda cr: needed(cr[0], cr[2]), visit, carry)

    @pl.when(seq + 1 < nseq)
    def _():
        for cp in copies(seq + 1, 0, 0):
            cp.start()

    ocat = jnp.concatenate([acc[h * ts:(h + 1) * ts] for h in range(SB_HEADS)], axis=1)
    o_ref[...] = _dot(ocat.astype(BF16), sel_ref[...])


def _sample_attn(kind, layer, pt, q, knew, extra, caches, npg):
    nseq, npages = pt.shape
    nch = npages // npg
    ts = q.shape[1] // nseq
    w = q.shape[-1]
    kw_ = knew.shape[-1]
    qspec = pl.BlockSpec((8, ts, w), lambda s, *_: (0, s, 0))
    knspec = pl.BlockSpec((1, PAGE, kw_), lambda s, *_: (s, 0, 0))
    ospec = pl.BlockSpec((ts, 512), lambda s, *_: (s, 0))
    full = lambda a: pl.BlockSpec(a.shape, lambda s, *_: (0,) * a.ndim)
    anyspec = pl.BlockSpec(memory_space=pl.ANY)
    out_shape = jax.ShapeDtypeStruct((nseq * ts, 512), F32)
    common = dict(layer=layer, nseq=nseq, nch=nch, npg=npg)
    if kind == "mla":
        ksnew, kspast, wuv = extra
        (rows_c,) = caches
        gs = pltpu.PrefetchScalarGridSpec(
            num_scalar_prefetch=1, grid=(nseq,),
            in_specs=[qspec, knspec, pl.BlockSpec((1, 8, PAGE), lambda s, *_: (s, 0, 0)),
                      pl.BlockSpec((1,) + kspast.shape[1:], lambda s, *_: (s, 0, 0, 0)), full(wuv), anyspec],
            out_specs=ospec,
            scratch_shapes=[pltpu.VMEM((2, npg, MLA_ROW, PAGE), F32), pltpu.SemaphoreType.DMA((2,))])
        return pl.pallas_call(functools.partial(_mla_sample_kernel, **common), out_shape=out_shape,
                              grid_spec=gs, compiler_params=_cparams(("arbitrary",)),
                              name="mla_sample")(pt, q, knew, ksnew, kspast, wuv, rows_c)
    if kind == "diff":
        lam, bias, bnew, g, lam_init = extra
        (rows_c,) = caches
        gs = pltpu.PrefetchScalarGridSpec(
            num_scalar_prefetch=2, grid=(nseq,),
            in_specs=[qspec, knspec, full(bias), full(bnew), full(g), anyspec],
            out_specs=ospec,
            scratch_shapes=[pltpu.VMEM((2, npg, PAGE, DIFF_ROW), F32), pltpu.SemaphoreType.DMA((2,))])
        return pl.pallas_call(functools.partial(_diff_sample_kernel, lam_init=lam_init, **common),
                              out_shape=out_shape, grid_spec=gs, compiler_params=_cparams(("arbitrary",)),
                              name="diff_sample")(pt, lam, q, knew, bias, bnew, g, rows_c)
    umat, uo, sel = extra
    (rows_c,) = caches
    gs = pltpu.PrefetchScalarGridSpec(
        num_scalar_prefetch=1, grid=(nseq,),
        in_specs=[qspec, knspec, full(umat), full(uo), full(sel), anyspec],
        out_specs=ospec,
        scratch_shapes=[pltpu.VMEM((2, npg, PAGE, SB_ROW), F32), pltpu.SemaphoreType.DMA((2,))])
    return pl.pallas_call(functools.partial(_sb_sample_kernel, **common), out_shape=out_shape,
                          grid_spec=gs, compiler_params=_cparams(("arbitrary",)),
                          name="sb_sample")(pt, q, knew, umat, uo, sel, rows_c)


def _merge_kernel(h_ref, g_ref, wg0, wg1, wg2, wg3, o_ref, wb_ref, out_ref, xn_sc):
    @pl.when(pl.program_id(1) == 0)
    def _():
        xn_sc[...] = _rms(h_ref[...], g_ref[...]).astype(BF16)

    xn = xn_sc[...]
    acc = None
    for g, wg in enumerate((wg0, wg1, wg2, wg3)):
        gate = jax.nn.sigmoid(_dot(xn, wg[...]))
        term = gate * _dot(o_ref[g], wb_ref[g])
        acc = term if acc is None else acc + term
    out_ref[...] = acc.astype(out_ref.dtype)


def _merge(h, g, wgates, o_all, wb):
    n, d = h.shape
    tm = _pick(n, 512)
    tn = _pick(d, 512)
    nj = d // tn
    bw = o_all.shape[-1]
    gspec = lambda gi: pl.BlockSpec((d, tn), lambda i, j: (0, gi * nj + j))
    return pl.pallas_call(
        _merge_kernel,
        out_shape=jax.ShapeDtypeStruct((n, d), BF16),
        grid=(n // tm, nj),
        in_specs=[pl.BlockSpec((tm, d), lambda i, j: (i, 0)),
                  pl.BlockSpec((1, d), lambda i, j: (0, 0)),
                  gspec(0), gspec(1), gspec(2), gspec(3),
                  pl.BlockSpec((N_BRANCH, tm, bw), lambda i, j: (0, i, 0)),
                  pl.BlockSpec((N_BRANCH, bw, tn), lambda i, j: (0, 0, j))],
        out_specs=pl.BlockSpec((tm, tn), lambda i, j: (i, j)),
        scratch_shapes=[pltpu.VMEM((tm, d), BF16)],
        compiler_params=_cparams(("arbitrary", "arbitrary")),
        name="merge",
    )(h, g, wgates, wgates, wgates, wgates, o_all, wb)


def _outproj_kernel(m_ref, w_ref, h_ref, o_ref):
    o_ref[...] = h_ref[...] + _dot(m_ref[...], w_ref[...])


def _outproj(merged, w, h):
    n, d = h.shape
    tm = _pick(n, 512)
    tn = _pick(d, 1024)
    return pl.pallas_call(
        _outproj_kernel,
        out_shape=jax.ShapeDtypeStruct((n, d), F32),
        grid=(n // tm, d // tn),
        in_specs=[pl.BlockSpec((tm, d), lambda i, j: (i, 0)),
                  pl.BlockSpec((d, tn), lambda i, j: (0, j)),
                  pl.BlockSpec((tm, tn), lambda i, j: (i, j))],
        out_specs=pl.BlockSpec((tm, tn), lambda i, j: (i, j)),
        compiler_params=_cparams(("arbitrary", "arbitrary")),
        name="outproj",
    )(merged, w, h)


def _ffn_kernel(h_ref, g_ref, wg_ref, wu_ref, wd_ref, o_ref, hn_sc, acc_sc):
    f = pl.program_id(1)

    @pl.when(f == 0)
    def _():
        h = h_ref[...]
        hn_sc[...] = _rms(h, g_ref[...]).astype(BF16)
        acc_sc[...] = h

    hn = hn_sc[...]
    a = jax.nn.silu(_dot(hn, wg_ref[...])) * _dot(hn, wu_ref[...])
    acc_sc[...] += _dot(a.astype(BF16), wd_ref[...])

    @pl.when(f == pl.num_programs(1) - 1)
    def _():
        o_ref[...] = acc_sc[...]


def _ffn(h, g, wg, wu, wd):
    n, d = h.shape
    dff = wg.shape[1]
    tm = _pick(n, 512)
    tf = 512 if dff % 512 == 0 else _pick(dff, 256)
    return pl.pallas_call(
        _ffn_kernel,
        out_shape=jax.ShapeDtypeStruct((n, d), F32),
        grid=(n // tm, dff // tf),
        in_specs=[pl.BlockSpec((tm, d), lambda i, f: (i, 0)),
                  pl.BlockSpec((1, d), lambda i, f: (0, 0)),
                  pl.BlockSpec((d, tf), lambda i, f: (0, f)),
                  pl.BlockSpec((d, tf), lambda i, f: (0, f)),
                  pl.BlockSpec((tf, d), lambda i, f: (f, 0))],
        out_specs=pl.BlockSpec((tm, d), lambda i, f: (i, 0)),
        scratch_shapes=[pltpu.VMEM((tm, d), BF16), pltpu.VMEM((tm, d), F32)],
        compiler_params=_cparams(("arbitrary", "arbitrary")),
        name="ffn",
    )(h, g, wg, wu, wd)


def _rope_tables(pos):
    half = MLA_D_ROPE // 2
    inv = ROPE_BASE ** (-jnp.arange(half, dtype=F32) / half)
    ang = pos.astype(F32)[:, None] * inv[None, :]
    reps = LANE // half
    return jnp.tile(jnp.cos(ang), (1, reps)), jnp.tile(jnp.sin(ang), (1, reps))


def _bias_by_distance(rel_bias, n):
    exact = REL_BUCKETS // 2
    n = jnp.maximum(n, 0)
    nf = jnp.maximum(n, 1).astype(F32)
    large = exact + (jnp.log(nf / exact) / math.log(REL_MAX_DIST / exact)
                     * (REL_BUCKETS - exact)).astype(jnp.int32)
    bucket = jnp.where(n < exact, n, jnp.minimum(large, REL_BUCKETS - 1))
    return jnp.moveaxis(rel_bias[bucket].astype(F32), -1, 0)


def _static_mats():
    col = np.arange(768)
    head = np.where(col < 512, col // 64, np.where(col < 640, (col - 512) // 16, (col - 640) // 16))
    bd96 = (head[:, None] == head[None, :]).astype(np.float32)
    c5 = np.arange(512)
    bd64 = (c5[:, None] // 64 == c5[None, :] // 64).astype(np.float32)
    s8 = np.zeros((512, LANE), np.float32)
    s8[c5, c5 // 64] = 1.0
    half = MLA_D_ROPE // 2
    selq = np.zeros((256, MLA_HEADS * MLA_QW), np.float32)
    for h in range(MLA_HEADS):
        for i in range(half):
            selq[h * half + i, h * MLA_QW + MLA_KV_RANK + i] = 1.0
            selq[128 + h * half + i, h * MLA_QW + MLA_KV_RANK + half + i] = 1.0
    selsb = np.zeros((SB_HEADS * LANE, 512), np.float32)
    for h in range(SB_HEADS):
        for d in range(SB_D):
            selsb[h * LANE + SB_D + d, h * SB_D + d] = 1.0
    return bd96, bd64, s8, selq, selsb


def _suffix_matrix(tk):
    j = np.arange(tk)
    return (j[:, None] > j[None, :]).astype(np.float32)


def _layer_consts(l, P):
    bd96, bd64, s8, selq, selsb = _static_mats()
    row = lambda v: v.reshape(1, -1).astype(F32)
    w_uq = P["mla_w_uq"][l]
    wuq = jnp.concatenate([w_uq[:, :, :MLA_D_NOPE].reshape(MLA_Q_RANK, 512),
                           w_uq[:, :, MLA_D_NOPE:MLA_D_NOPE + 16].reshape(MLA_Q_RANK, 128),
                           w_uq[:, :, MLA_D_NOPE + 16:].reshape(MLA_Q_RANK, 128)], axis=1).astype(BF16)
    qn = P["mla_qn_g"][l]
    qng = row(jnp.concatenate([jnp.tile(qn[:64], 8), jnp.tile(qn[64:80], 8), jnp.tile(qn[80:], 8)]))
    kn = P["mla_kn_g"][l]
    kng = row(jnp.tile(kn[:MLA_D_NOPE], 8))
    knr = row(jnp.concatenate([kn[MLA_D_NOPE:], jnp.zeros((LANE - MLA_D_ROPE,), F32)]))
    w_uk = P["mla_w_uk"][l]
    blk = jnp.pad(jnp.transpose(w_uk, (1, 2, 0)), ((0, 0), (0, 0), (0, MLA_QW - MLA_KV_RANK)))
    m_nope = jnp.einsum("hjc,hg->hjgc", blk, jnp.eye(MLA_HEADS, dtype=F32)).reshape(512, MLA_HEADS * MLA_QW)
    mq = jnp.concatenate([m_nope, jnp.asarray(selq)], axis=0).astype(BF16)
    wuk = w_uk.reshape(MLA_KV_RANK, 512).astype(BF16)
    tok_consts = [row(P["mla_cq_g"][l]), row(P["mla_ckv_g"][l]), wuq, qng, jnp.asarray(bd96, BF16),
                  kng, knr, wuk, jnp.asarray(s8, BF16),
                  row(jnp.tile(P["diff_qn_g"][l], 8)), jnp.asarray(bd64, BF16),
                  row(jnp.tile(P["diff_kn_g"][l], 2))]
    wuv = jnp.einsum("chd,hg->hcgd", P["mla_w_uv"][l], jnp.eye(MLA_HEADS, dtype=F32))
    wuv = wuv.reshape(MLA_HEADS * MLA_KV_RANK, 512).astype(BF16)
    return tok_consts, mq, wuv, jnp.asarray(selsb, BF16)


def _in_weights(w):
    d = w.shape[0]
    small = jnp.concatenate([w[:, 0:512], w[:, 544:1056], w[:, 1056:1568], w[:, 1824:2336],
                             w[:, 512:544], jnp.zeros((d, LANE - MLA_D_ROPE), w.dtype),
                             w[:, 1568:1824], w[:, 2336:2464]], axis=1).astype(BF16)
    return small, w[:, 2464:].astype(BF16)


def kernel(x_prompt, x_sample, cache_mla, cache_mla_kscale, cache_diff, cache_sb, state_pool,
           page_table, norm1_g, w_in, mla_cq_g, mla_ckv_g, mla_w_uq, mla_qn_g, mla_kn_g,
           mla_w_uk, mla_w_uv, pool_w, pool_scale, diff_qn_g, diff_kn_g, diff_lambda,
           diff_subln_g, rel_bias, w_branch, w_out, norm2_g, ffn_w_gate, ffn_w_up, ffn_w_down):
    P = dict(mla_cq_g=mla_cq_g, mla_ckv_g=mla_ckv_g, mla_w_uq=mla_w_uq, mla_qn_g=mla_qn_g,
             mla_kn_g=mla_kn_g, mla_w_uk=mla_w_uk, mla_w_uv=mla_w_uv, diff_qn_g=diff_qn_g,
             diff_kn_g=diff_kn_g)
    b, t, d = x_prompt.shape
    bs, ts, _ = x_sample.shape
    depth = w_in.shape[0]
    npages = page_table.shape[1]
    n_past = npages * PAGE
    n_p, n_s = b * t, bs * ts
    row = lambda v: v.reshape(1, -1).astype(F32)

    tq = _pick(t, 128)
    tk = _pick(t, 256)
    npg = _pick(npages, 16)

    cos_p, sin_p = _rope_tables(jnp.arange(t))
    cos_s, sin_s = _rope_tables(n_past + jnp.arange(ts))
    cos_s, sin_s = jnp.tile(cos_s, (bs, 1)), jnp.tile(sin_s, (bs, 1))

    ncase = -(-(tk + 112) // tq)
    ii = jnp.arange(tq)[None, :]
    jj = jnp.arange(tk)[:, None]
    dist_p = jnp.stack([dcase * tq + ii - jj for dcase in range(ncase)] +
                       [jnp.full((tk, tq), REL_MAX_DIST, jnp.int32)])
    bias_p = jnp.moveaxis(_bias_by_distance(rel_bias, dist_p), 0, 1)
    tkc = npg * PAGE
    tt = jnp.arange(ts)[:, None]
    dist_last = n_past + tt - (n_past - tkc + jnp.arange(tkc)[None, :])
    dist_s = jnp.stack([jnp.full((ts, tkc), REL_MAX_DIST + tkc, jnp.int32), dist_last])
    bias_s = jnp.moveaxis(_bias_by_distance(rel_bias, dist_s), 0, 1)
    bias_new = _bias_by_distance(rel_bias, tt - jnp.arange(PAGE)[None, :])

    lmat_p = jnp.asarray(_suffix_matrix(tk).T, BF16)
    umat_s = jnp.asarray(_suffix_matrix(PAGE), BF16)
    uo_s = jnp.asarray(np.concatenate([_suffix_matrix(PAGE), np.ones((PAGE, PAGE), np.float32)], axis=1), BF16)
    e1 = np.zeros((LANE, MLA_HEADS * tq), np.float32)
    for hd in range(MLA_HEADS):
        e1[hd, hd * tq:(hd + 1) * tq] = 1.0
    e3 = jnp.asarray(np.concatenate([e1, e1, e1], axis=0), BF16)

    cache_mla_t = jnp.swapaxes(cache_mla, 2, 3)

    h = jnp.concatenate([x_prompt.reshape(n_p, d), x_sample.reshape(n_s, d)], axis=0)
    outs = {k: [] for k in ("mla_p", "ks_p", "diff_p", "sb_p", "pool_p",
                            "mla_s", "ks_s", "diff_s", "sb_s", "pool_s")}
    for l in range(depth):
        tok_consts, mq, wuv, selsb = _layer_consts(l, P)
        w_small, w_gates = _in_weights(w_in[l])
        lq1, lk1, lq2, lk2 = diff_lambda[l].astype(F32)
        lam_init = 0.8 - 0.6 * math.exp(-0.3 * l)
        lam = (jnp.exp(jnp.sum(lq1 * lk1)) - jnp.exp(jnp.sum(lq2 * lk2)) + lam_init).reshape(1)
        subln = row(diff_subln_g[l])

        proj = _inproj(h, row(norm1_g[l]), w_small)

        (qm_p, qd_p, qs_p, mrow_p, ks_p, drow_p, km_p, ct_p, ks3_p, kd_p, dvt_p, kb_p, sbt_p) = _tok_prompt(
            proj, b, t, tk, cos_p, sin_p, tok_consts + [mq.T])
        (qm_s, qd_s, qs_s, mrow_s, ks_s, drow_s, km_s, kd_s, kb_s) = _tok_sample(
            proj, n_p, n_s, cos_s, sin_s, tok_consts + [mq])

        pw = pool_w[l].astype(BF16)
        psc = row(pool_scale[l])
        o_pool_p = _pool_prompt(proj, b, t, pw, psc)
        u_s = proj[n_p:, C_POOL:C_DQ].reshape(bs, ts, 512)
        ext_s = jnp.concatenate([state_pool[l], u_s], axis=1)
        o_pool_s = _pool_sample(jnp.swapaxes(ext_s, 0, 1), pw, psc, n_past)
        o_pool_s = jnp.swapaxes(o_pool_s, 0, 1).reshape(n_s, 512)

        g_tile = jnp.broadcast_to(diff_subln_g[l].astype(F32)[:, None], (DIFF_DV, tq))
        o_mla_p = _prompt_attn("mla", qm_p, km_p, ct_p, (ks3_p, e3, wuv.T), b, t, tq, tk)
        o_diff_p = _prompt_attn("diff", qd_p, kd_p, dvt_p, (lam, bias_p, g_tile), b, t, tq, tk,
                                lam_init=lam_init)
        o_sb_p = _prompt_attn("sb", qs_p, kb_p, sbt_p, (lmat_p,), b, t, tq, tk)

        pad_new = lambda a: jnp.pad(a.reshape(bs, ts, -1), ((0, 0), (0, PAGE - ts), (0, 0))).astype(BF16)
        ksnew = jnp.pad(jnp.swapaxes(ks_s.reshape(bs, ts, MLA_HEADS), 1, 2), ((0, 0), (0, 0), (0, PAGE - ts)))
        kspast = jnp.transpose(cache_mla_kscale[l][page_table].reshape(bs, npages // npg, npg * PAGE, MLA_HEADS),
                               (0, 1, 3, 2))
        o_mla_s = _sample_attn("mla", l, page_table, qm_s[0], pad_new(km_s), (ksnew, kspast, wuv),
                               (cache_mla_t,), npg)
        o_diff_s = _sample_attn("diff", l, page_table, qd_s[0], pad_new(kd_s),
                                (lam, bias_s, bias_new, subln, lam_init), (cache_diff,), npg)
        o_sb_s = _sample_attn("sb", l, page_table, qs_s[0], pad_new(kb_s), (umat_s, uo_s, selsb),
                              (cache_sb,), _pick(npages, 4))

        cat = lambda a, c: jnp.concatenate([a, c.astype(BF16)], axis=0)
        o_all = jnp.stack([cat(o_mla_p, o_mla_s), cat(o_pool_p, o_pool_s),
                           cat(o_diff_p, o_diff_s), cat(o_sb_p, o_sb_s)])

        merged = _merge(h, row(norm1_g[l]), w_gates, o_all, w_branch[l].astype(BF16))
        h = _outproj(merged, w_out[l].astype(BF16), h)
        h = _ffn(h, row(norm2_g[l]), ffn_w_gate[l].astype(BF16), ffn_w_up[l].astype(BF16),
                 ffn_w_down[l].astype(BF16))

        u_p = proj[:n_p, C_POOL:C_DQ].reshape(b, t, 512)
        outs["mla_p"].append(mrow_p.reshape(b, t, MLA_ROW))
        outs["ks_p"].append(ks_p.reshape(b, t, MLA_HEADS))
        outs["diff_p"].append(drow_p.reshape(b, t, DIFF_ROW))
        outs["sb_p"].append(proj[:n_p, C_SKV:C_END].reshape(b, t, SB_ROW))
        outs["pool_p"].append(jnp.concatenate([jnp.zeros((b, POOL_PAD, 512), F32), u_p], axis=1)[:, -POOL_PAD:])
        outs["mla_s"].append(mrow_s.reshape(bs, ts, MLA_ROW))
        outs["ks_s"].append(ks_s.reshape(bs, ts, MLA_HEADS))
        outs["diff_s"].append(drow_s.reshape(bs, ts, DIFF_ROW))
        outs["sb_s"].append(proj[n_p:, C_SKV:C_END].reshape(bs, ts, SB_ROW))
        outs["pool_s"].append(ext_s[:, -POOL_PAD:])

    st = lambda k: jnp.stack(outs[k])
    return (h[:n_p].reshape(b, t, d), h[n_p:].reshape(bs, ts, d),
            st("mla_p"), st("ks_p"), st("diff_p"), st("sb_p"), st("pool_p"),
            st("mla_s"), st("ks_s"), st("diff_s"), st("sb_s"), st("pool_s"))
```
